```python
import jax, jax.numpy as jnp
from jax import lax
import numpy as np

D_MODEL = 1024
BATCH = 16
SEQ = 4096
DEPTH = 2

D_MIX = D_MODEL
CA = D_MIX // 2
A_GROUPS = 8
CONV_WIDTH = 31
CB = D_MIX // 2
GB = 4
DB = CB // GB
CHUNK = 128
N_HEADS = 16
HEAD_DIM = D_MIX // N_HEADS
Q_BLOCK = 128
D_FF = 2816
N_EVEN = (DEPTH + 1) // 2
N_ODD = DEPTH // 2
EPS = 1e-6

kernel_name = "hybrid_conv_gmlp_stickbreaking_macaron"


def rms_norm(x, g):
    xf = x.astype(jnp.float32)
    y = xf * lax.rsqrt(jnp.mean(xf * xf, axis=-1, keepdims=True) + EPS)
    return (y * g.astype(jnp.float32)).astype(x.dtype)


def layer_norm(x, g, b):
    xf = x.astype(jnp.float32)
    mu = jnp.mean(xf, axis=-1, keepdims=True)
    xc = xf - mu
    var = jnp.mean(xc * xc, axis=-1, keepdims=True)
    y = xc * lax.rsqrt(var + EPS)
    return (y * g.astype(jnp.float32) + b.astype(jnp.float32)).astype(x.dtype)


def swiglu(h, w_gate, w_up, w_down):
    return (jax.nn.silu(h @ w_gate) * (h @ w_up)) @ w_down


def causal_depthwise_conv(x, w, b):
    k, c = w.shape
    y = lax.conv_general_dilated(
        x, w[:, None, :], window_strides=(1,), padding=[(k - 1, 0)],
        dimension_numbers=("NWC", "WIO", "NWC"), feature_group_count=c)
    return y + b


def conv_gating_mixer(h, w_in, conv_w, conv_b, ln_a_g, ln_a_b, ln_v_g, ln_v_b,
                      sp_w, sp_b, w_out):
    bsz, s, _ = h.shape
    z = h @ w_in
    a_val, a_gate, u, v = jnp.split(z, [CA, 2 * CA, 2 * CA + CB], axis=-1)
    a = a_val * jax.nn.sigmoid(a_gate)
    a = causal_depthwise_conv(a, conv_w, conv_b)
    a = jax.nn.silu(layer_norm(a, ln_a_g, ln_a_b))
    v = layer_norm(v.reshape(bsz, s, GB, DB), ln_v_g, ln_v_b)
    v = v.reshape(bsz, s // CHUNK, CHUNK, GB, DB)
    causal = jnp.tril(jnp.ones((CHUNK, CHUNK), dtype=bool))
    w_s = jnp.where(causal[None], sp_w, 0)
    v = jnp.einsum("gts,bcsgd->bctgd", w_s, v) + sp_b.T[:, :, None]
    b_out = u * v.reshape(bsz, s, CB)
    return jnp.concatenate([a, b_out], axis=-1) @ w_out


def stick_breaking_attention(q, k, v):
    bsz, s, h, dh = q.shape
    nb = s // Q_BLOCK
    scale = dh ** -0.5

    def to_blocks(t):
        return t.reshape(bsz, nb, Q_BLOCK, h, dh).transpose(1, 0, 3, 2, 4)

    qb, kb, vb = to_blocks(q), to_blocks(k), to_blocks(v)
    pos = jnp.arange(Q_BLOCK)

    def per_query_block(args):
        qi, qblk = args
        qf = qblk.astype(jnp.float32)

        def step(carry, xs):
            kj, kblk, vblk = xs

            def accumulate(c):
                log_rem, acc = c
                z = jnp.einsum("bhtd,bhsd->bhts", qf, kblk.astype(jnp.float32)) * scale
                mask = (kj * Q_BLOCK + pos[None, :]) < (qi * Q_BLOCK + pos[:, None])
                log_keep = jnp.where(mask, -jax.nn.softplus(z), 0.0)
                suffix = lax.cumsum(log_keep, axis=3, reverse=True) - log_keep
                log_a = jax.nn.log_sigmoid(z) + suffix + log_rem[..., None]
                a = jnp.where(mask, jnp.exp(log_a), 0.0)
                acc = acc + jnp.einsum("bhts,bhsd->bhtd", a, vblk.astype(jnp.float32))
                return (log_rem + jnp.sum(log_keep, axis=-1), acc)

            return lax.cond(kj <= qi, accumulate, lambda c: c, carry), None

        init = (jnp.zeros((bsz, h, Q_BLOCK), jnp.float32),
                jnp.zeros((bsz, h, Q_BLOCK, dh), jnp.float32))
        (_, acc), _ = lax.scan(step, init, (jnp.arange(nb), kb, vb), reverse=True)
        return acc.astype(q.dtype)

    out = lax.map(per_query_block, (jnp.arange(nb), qb))
    return out.transpose(1, 0, 3, 2, 4).reshape(bsz, s, h * dh)


def stick_breaking_mixer(h, w_qkv, w_o):
    bsz, s, _ = h.shape
    q, k, v = jnp.split(h @ w_qkv, 3, axis=-1)
    q = q.reshape(bsz, s, N_HEADS, HEAD_DIM)
    k = k.reshape(bsz, s, N_HEADS, HEAD_DIM)
    v = v.reshape(bsz, s, N_HEADS, HEAD_DIM)
    return stick_breaking_attention(q, k, v) @ w_o


def setup_inputs(seed: int = 0) -> dict:
    key = jax.random.key(seed)
    ks = iter(jax.random.split(key, 32))
    f32 = jnp.float32

    def nrm(shape, fan_in):
        return jax.random.normal(next(ks), shape, f32) * (fan_in ** -0.5)

    def gain(shape):
        return 1.0 + 0.02 * jax.random.normal(next(ks), shape, f32)

    def bias(shape, s=0.02):
        return s * jax.random.normal(next(ks), shape, f32)

    return {
        "x": jax.random.normal(next(ks), (BATCH, SEQ, D_MODEL), f32),
        "g_ffn1": gain((DEPTH, D_MODEL)),
        "w_ffn1_gate": nrm((DEPTH, D_MODEL, D_FF), D_MODEL),
        "w_ffn1_up": nrm((DEPTH, D_MODEL, D_FF), D_MODEL),
        "w_ffn1_down": nrm((DEPTH, D_FF, D_MODEL), D_FF),
        "g_mix": gain((DEPTH, D_MODEL)),
        "w_in_ab": nrm((N_EVEN, D_MODEL, 2 * CA + 2 * CB), D_MODEL),
        "conv_w": nrm((N_EVEN, CONV_WIDTH, CA), CONV_WIDTH),
        "conv_b": bias((N_EVEN, CA)),
        "ln_a_g": gain((N_EVEN, CA)),
        "ln_a_b": bias((N_EVEN, CA)),
        "ln_v_g": gain((N_EVEN, GB, DB)),
        "ln_v_b": bias((N_EVEN, GB, DB)),
        "sp_w": nrm((N_EVEN, GB, CHUNK, CHUNK), CHUNK),
        "sp_b": 1.0 + bias((N_EVEN, GB, CHUNK), 0.1),
        "w_out_ab": nrm((N_EVEN, CA + CB, D_MODEL), CA + CB),
        "w_qkv": nrm((N_ODD, D_MODEL, 3 * D_MIX), D_MODEL),
        "w_o": nrm((N_ODD, D_MIX, D_MODEL), D_MIX),
        "g_ffn2": gain((DEPTH, D_MODEL)),
        "w_ffn2_gate": nrm((DEPTH, D_MODEL, D_FF), D_MODEL),
        "w_ffn2_up": nrm((DEPTH, D_MODEL, D_FF), D_MODEL),
        "w_ffn2_down": nrm((DEPTH, D_FF, D_MODEL), D_FF),
        "g_final": gain((D_MODEL,)),
    }


def reference(x, g_ffn1, w_ffn1_gate, w_ffn1_up, w_ffn1_down, g_mix, w_in_ab,
              conv_w, conv_b, ln_a_g, ln_a_b, ln_v_g, ln_v_b, sp_w, sp_b, w_out_ab,
              w_qkv, w_o, g_ffn2, w_ffn2_gate, w_ffn2_up, w_ffn2_down, g_final):
    for l in range(DEPTH):
        x = x + 0.5 * swiglu(rms_norm(x, g_ffn1[l]), w_ffn1_gate[l], w_ffn1_up[l], w_ffn1_down[l])
        h = rms_norm(x, g_mix[l])
        i = l // 2
        if l % 2 == 0:
            x = x + conv_gating_mixer(h, w_in_ab[i], conv_w[i], conv_b[i], ln_a_g[i], ln_a_b[i],
                                      ln_v_g[i], ln_v_b[i], sp_w[i], sp_b[i], w_out_ab[i])
        else:
            x = x + stick_breaking_mixer(h, w_qkv[i], w_o[i])
        x = x + 0.5 * swiglu(rms_norm(x, g_ffn2[l]), w_ffn2_gate[l], w_ffn2_up[l], w_ffn2_down[l])
    return rms_norm(x, g_final)
```

```python
import functools

import jax
import jax.numpy as jnp
from jax import lax
from jax.experimental import pallas as pl
from jax.experimental.pallas import tpu as pltpu

EPS = 1e-6
LANES = 128
CONV_HALO = 32
GROUPS_B = 4
CHUNK = 128
HEAD_DIM = 64
ATTN_BLOCK = 256
VMEM_LIMIT = 56 * 1024 * 1024

F32 = jnp.float32
BF16 = jnp.bfloat16


def _resident(shape):
    zeros = (0,) * len(shape)
    return pl.BlockSpec(shape, lambda *_: zeros, pipeline_mode=pl.Buffered(1))


def _rms(x, g):
    ms = jnp.mean(x * x, axis=-1, keepdims=True)
    return x * lax.rsqrt(ms + EPS) * g


def _ffn_kernel(x_ref, g_ref, wg_ref, wu_ref, wd_ref, gf_ref, o_ref, act_ref, *, tf, final_norm):
    x = x_ref[...]
    h = _rms(x, g_ref[...]).astype(BF16)
    d_ff = wg_ref.shape[1]
    for j in range(d_ff // tf):
        cols = slice(j * tf, (j + 1) * tf)
        gate = jnp.dot(h, wg_ref[:, cols], preferred_element_type=F32)
        up = jnp.dot(h, wu_ref[:, cols], preferred_element_type=F32)
        act_ref[:, cols] = (gate * jax.nn.sigmoid(gate) * up).astype(BF16)
    y = jnp.dot(act_ref[...], wd_ref[...], preferred_element_type=F32)
    out = x + 0.5 * y
    if final_norm:
        out = _rms(out, gf_ref[...])
    o_ref[...] = out


def _ffn(x2, g, wg, wu, wd, g_final, *, final_norm, tm=512, tf=256):
    m, d = x2.shape
    d_ff = wg.shape[1]
    kern = functools.partial(_ffn_kernel, tf=tf, final_norm=final_norm)
    return pl.pallas_call(
        kern,
        out_shape=jax.ShapeDtypeStruct((m, d), F32),
        grid=(m // tm,),
        in_specs=[
            pl.BlockSpec((tm, d), lambda i: (i, 0)),
            _resident((1, d)),
            _resident((d, d_ff)),
            _resident((d, d_ff)),
            _resident((d_ff, d)),
            _resident((1, d)),
        ],
        out_specs=pl.BlockSpec((tm, d), lambda i: (i, 0)),
        scratch_shapes=[pltpu.VMEM((tm, d_ff), BF16)],
        compiler_params=pltpu.CompilerParams(
            dimension_semantics=("arbitrary",), vmem_limit_bytes=VMEM_LIMIT),
        name="ffn",
    )(x2, g.reshape(1, d), wg, wu, wd, g_final.reshape(1, d))


def _ab_in_kernel(x_ref, g_ref, w_ref, lvg_ref, lvb_ref, a_ref, u_ref, v_ref):
    h = _rms(x_ref[...], g_ref[...]).astype(BF16)
    z = jnp.dot(h, w_ref[...], preferred_element_type=F32)
    c = a_ref.shape[1]
    a_ref[...] = z[:, :c] * jax.nn.sigmoid(z[:, c:2 * c])
    u_ref[...] = z[:, 2 * c:3 * c]
    lvg = lvg_ref[...]
    lvb = lvb_ref[...]
    for grp in range(GROUPS_B):
        lo = 3 * c + grp * LANES
        vg = z[:, lo:lo + LANES]
        mu = jnp.mean(vg, axis=-1, keepdims=True)
        xc = vg - mu
        var = jnp.mean(xc * xc, axis=-1, keepdims=True)
        sl = slice(grp * LANES, (grp + 1) * LANES)
        v_ref[:, sl] = (xc * lax.rsqrt(var + EPS) * lvg[:, sl] + lvb[:, sl]).astype(BF16)


def _ab_in(x2, g, w_in, ln_v_g, ln_v_b, *, tm=512):
    m, d = x2.shape
    n = w_in.shape[1]
    c = n // 4
    return pl.pallas_call(
        _ab_in_kernel,
        out_shape=(jax.ShapeDtypeStruct((m, c), F32),
                   jax.ShapeDtypeStruct((m, c), F32),
                   jax.ShapeDtypeStruct((m, c), BF16)),
        grid=(m // tm,),
        in_specs=[
            pl.BlockSpec((tm, d), lambda i: (i, 0)),
            _resident((1, d)),
            _resident((d, n)),
            _resident((1, c)),
            _resident((1, c)),
        ],
        out_specs=(pl.BlockSpec((tm, c), lambda i: (i, 0)),
                   pl.BlockSpec((tm, c), lambda i: (i, 0)),
                   pl.BlockSpec((tm, c), lambda i: (i, 0))),
        compiler_params=pltpu.CompilerParams(
            dimension_semantics=("arbitrary",), vmem_limit_bytes=VMEM_LIMIT),
        name="ab_in",
    )(x2, g.reshape(1, d), w_in, ln_v_g.reshape(1, c), ln_v_b.reshape(1, c))


def _ab_mix_kernel(ac_ref, ap_ref, u_ref, v_ref, x_ref, cw_ref, cb_ref, lag_ref, lab_ref,
                   spw_ref, spb_ref, wo_ref, o_ref, abuf_ref, m_ref):
    ts, c = ac_ref.shape[1], ac_ref.shape[2]
    ktaps = cw_ref.shape[0]
    first = pl.program_id(1) == 0
    abuf_ref[0:CONV_HALO, :] = jnp.where(first, 0.0, ap_ref[0])
    abuf_ref[CONV_HALO:, :] = ac_ref[0]
    conv = jnp.zeros((ts, c), F32) + cb_ref[...]
    for k in range(ktaps):
        off = CONV_HALO - (ktaps - 1) + k
        conv = conv + cw_ref[k:k + 1, :] * abuf_ref[off:off + ts, :]
    mu = jnp.mean(conv, axis=-1, keepdims=True)
    xc = conv - mu
    var = jnp.mean(xc * xc, axis=-1, keepdims=True)
    a = xc * lax.rsqrt(var + EPS) * lag_ref[...] + lab_ref[...]
    m_ref[:, 0:c] = (a * jax.nn.sigmoid(a)).astype(BF16)

    row = lax.broadcasted_iota(jnp.int32, (CHUNK, CHUNK), 0)
    col = lax.broadcasted_iota(jnp.int32, (CHUNK, CHUNK), 1)
    causal = row >= col
    for grp in range(GROUPS_B):
        w = jnp.where(causal, spw_ref[grp], 0.0).astype(BF16)
        bias = spb_ref[:, grp:grp + 1]
        sl = slice(grp * LANES, (grp + 1) * LANES)
        for ch in range(ts // CHUNK):
            rows = slice(ch * CHUNK, (ch + 1) * CHUNK)
            mixed = jnp.dot(w, v_ref[0, rows, sl], preferred_element_type=F32) + bias
            m_ref[rows, c + grp * LANES:c + (grp + 1) * LANES] = (
                u_ref[0, rows, sl] * mixed).astype(BF16)

    y = jnp.dot(m_ref[...], wo_ref[...], preferred_element_type=F32)
    o_ref[0] = x_ref[0] + y


def _ab_mix(a, u, v, x3, conv_w, conv_b, ln_a_g, ln_a_b, sp_w, sp_b, w_out, *, ts=512):
    b, s, d = x3.shape
    c = a.shape[-1]
    ktaps = conv_w.shape[0]
    assert ktaps - 1 <= CONV_HALO and ts % CONV_HALO == 0 and ts % CHUNK == 0
    halo_blocks = ts // CONV_HALO
    return pl.pallas_call(
        _ab_mix_kernel,
        out_shape=jax.ShapeDtypeStruct((b, s, d), F32),
        grid=(b, s // ts),
        in_specs=[
            pl.BlockSpec((1, ts, c), lambda bi, i: (bi, i, 0)),
            pl.BlockSpec((1, CONV_HALO, c),
                         lambda bi, i: (bi, jnp.maximum(i * halo_blocks - 1, 0), 0)),
            pl.BlockSpec((1, ts, c), lambda bi, i: (bi, i, 0)),
            pl.BlockSpec((1, ts, c), lambda bi, i: (bi, i, 0)),
            pl.BlockSpec((1, ts, d), lambda bi, i: (bi, i, 0)),
            _resident((ktaps, c)),
            _resident((1, c)),
            _resident((1, c)),
            _resident((1, c)),
            _resident((GROUPS_B, CHUNK, CHUNK)),
            _resident((CHUNK, GROUPS_B)),
            _resident((2 * c, d)),
        ],
        out_specs=pl.BlockSpec((1, ts, d), lambda bi, i: (bi, i, 0)),
        scratch_shapes=[pltpu.VMEM((ts + CONV_HALO, c), F32),
                        pltpu.VMEM((ts, 2 * c), BF16)],
        compiler_params=pltpu.CompilerParams(
            dimension_semantics=("arbitrary", "arbitrary"), vmem_limit_bytes=VMEM_LIMIT),
        name="ab_mix",
    )(a.reshape(b, s, c), a.reshape(b, s, c), u.reshape(b, s, c), v.reshape(b, s, c), x3,
      conv_w, conv_b.reshape(1, c), ln_a_g.reshape(1, c), ln_a_b.reshape(1, c),
      sp_w, sp_b.T, w_out)


def _qkv_kernel(x_ref, g_ref, w_ref, o_ref, *, d_mix):
    h = _rms(x_ref[...], g_ref[...]).astype(BF16)
    z = jnp.dot(h, w_ref[...], preferred_element_type=F32)
    o_ref[:, :d_mix] = (z[:, :d_mix] * (HEAD_DIM ** -0.5)).astype(BF16)
    o_ref[:, d_mix:] = z[:, d_mix:].astype(BF16)


def _qkv(x2, g, w_qkv, *, tm=512):
    m, d = x2.shape
    n = w_qkv.shape[1]
    return pl.pallas_call(
        functools.partial(_qkv_kernel, d_mix=n // 3),
        out_shape=jax.ShapeDtypeStruct((m, n), BF16),
        grid=(m // tm,),
        in_specs=[
            pl.BlockSpec((tm, d), lambda i: (i, 0)),
            _resident((1, d)),
            _resident((d, n)),
        ],
        out_specs=pl.BlockSpec((tm, n), lambda i: (i, 0)),
        compiler_params=pltpu.CompilerParams(
            dimension_semantics=("arbitrary",), vmem_limit_bytes=VMEM_LIMIT),
        name="qkv",
    )(x2, g.reshape(1, d), w_qkv)


def _attn_kernel(q_ref, k_ref, v_ref, o_ref, acc_ref, rem_ref):
    tq = q_ref.shape[1]
    tk = ATTN_BLOCK
    i = pl.program_id(2)
    q2 = q_ref[0]
    lane = lax.broadcasted_iota(jnp.int32, (tq, LANES), 1)
    zero = jnp.zeros_like(q2)
    q_heads = (jnp.where(lane < HEAD_DIM, q2, zero), jnp.where(lane >= HEAD_DIM, q2, zero))
    kr = lax.broadcasted_iota(jnp.int32, (tk, tk), 0)
    kc = lax.broadcasted_iota(jnp.int32, (tk, tk), 1)
    tri = jnp.where(kr >= kc, 1.0, 0.0).astype(BF16)
    acc_ref[...] = jnp.zeros_like(acc_ref)
    rem_ref[...] = jnp.zeros_like(rem_ref)

    def block(j, strict_mask):
        start = pl.multiple_of(j * tk, tk)
        k_sb = k_ref[0, pl.ds(start, tk), :]
        v_sb = v_ref[0, pl.ds(start, tk), :]
        for hd in range(2):
            z = lax.dot_general(q_heads[hd], k_sb, (((1,), (1,)), ((), ())),
                                preferred_element_type=F32)
            nz = -z
            log_keep = jnp.minimum(nz, 0.0) - jnp.log(1.0 + jnp.exp(jnp.minimum(z, nz)))
            if strict_mask is not None:
                log_keep = jnp.where(strict_mask, log_keep, 0.0)
            hi = log_keep.astype(BF16)
            lo = (log_keep - hi.astype(F32)).astype(BF16)
            incl = (jnp.dot(hi, tri, preferred_element_type=F32)
                    + jnp.dot(lo, tri, preferred_element_type=F32))
            rem = rem_ref[hd]
            a = jnp.exp(z + incl + jnp.concatenate([rem, rem], axis=1))
            if strict_mask is not None:
                a = jnp.where(strict_mask, a, 0.0)
            acc_ref[hd] += jnp.dot(a.astype(BF16), v_sb, preferred_element_type=F32)
            rem_ref[hd] = rem + jnp.broadcast_to(incl[:, 0:1], (tq, LANES))

    qr = lax.broadcasted_iota(jnp.int32, (tq, tk), 0)
    qc = lax.broadcasted_iota(jnp.int32, (tq, tk), 1)
    block(i, qc < qr)

    def body(jj, carry):
        block(i - 1 - jj, None)
        return carry

    lax.fori_loop(0, i, body, 0)
    o_ref[0] = jnp.where(lane < HEAD_DIM, acc_ref[0], acc_ref[1]).astype(o_ref.dtype)


def _attention(qkv3, d_mix):
    b, s, _ = qkv3.shape
    tq = ATTN_BLOCK
    pairs = d_mix // LANES
    return pl.pallas_call(
        _attn_kernel,
        out_shape=jax.ShapeDtypeStruct((b, s, d_mix), BF16),
        grid=(b, pairs, s // tq),
        in_specs=[
            pl.BlockSpec((1, tq, LANES), lambda bi, p, i: (bi, i, p)),
            pl.BlockSpec((1, s, LANES), lambda bi, p, i: (bi, 0, pairs + p)),
            pl.BlockSpec((1, s, LANES), lambda bi, p, i: (bi, 0, 2 * pairs + p)),
        ],
        out_specs=pl.BlockSpec((1, tq, LANES), lambda bi, p, i: (bi, i, p)),
        scratch_shapes=[pltpu.VMEM((2, tq, LANES), F32), pltpu.VMEM((2, tq, LANES), F32)],
        compiler_params=pltpu.CompilerParams(
            dimension_semantics=("arbitrary", "arbitrary", "arbitrary"),
            vmem_limit_bytes=VMEM_LIMIT),
        name="sb_attn",
    )(qkv3, qkv3, qkv3)


def _proj_res_kernel(a_ref, w_ref, x_ref, o_ref):
    o_ref[...] = x_ref[...] + jnp.dot(a_ref[...], w_ref[...], preferred_element_type=F32)


def _proj_res(a2, w, x2, *, tm=512):
    m, d = x2.shape
    kdim = a2.shape[1]
    return pl.pallas_call(
        _proj_res_kernel,
        out_shape=jax.ShapeDtypeStruct((m, d), F32),
        grid=(m // tm,),
        in_specs=[
            pl.BlockSpec((tm, kdim), lambda i: (i, 0)),
            _resident((kdim, d)),
            pl.BlockSpec((tm, d), lambda i: (i, 0)),
        ],
        out_specs=pl.BlockSpec((tm, d), lambda i: (i, 0)),
        compiler_params=pltpu.CompilerParams(
            dimension_semantics=("arbitrary",), vmem_limit_bytes=VMEM_LIMIT),
        name="proj_res",
    )(a2, w, x2)


def kernel(x, g_ffn1, w_ffn1_gate, w_ffn1_up, w_ffn1_down, g_mix, w_in_ab, conv_w, conv_b, ln_a_g, ln_a_b, ln_v_g, ln_v_b, sp_w, sp_b, w_out_ab, w_qkv, w_o, g_ffn2, w_ffn2_gate, w_ffn2_up, w_ffn2_down, g_final):
    bsz, seq, d = x.shape
    depth = g_ffn1.shape[0]
    x2 = x.reshape(bsz * seq, d)
    bf = lambda w: w.astype(BF16)
    for l in range(depth):
        x2 = _ffn(x2, g_ffn1[l], bf(w_ffn1_gate[l]), bf(w_ffn1_up[l]), bf(w_ffn1_down[l]),
                  g_final, final_norm=False)
        i = l // 2
        if l % 2 == 0:
            a, u, v = _ab_in(x2, g_mix[l], bf(w_in_ab[i]), ln_v_g[i], ln_v_b[i])
            x3 = _ab_mix(a, u, v, x2.reshape(bsz, seq, d), conv_w[i], conv_b[i], ln_a_g[i],
                         ln_a_b[i], sp_w[i], sp_b[i], bf(w_out_ab[i]))
            x2 = x3.reshape(bsz * seq, d)
        else:
            d_mix = w_qkv.shape[2] // 3
            qkv = _qkv(x2, g_mix[l], bf(w_qkv[i]))
            attn = _attention(qkv.reshape(bsz, seq, 3 * d_mix), d_mix)
            x2 = _proj_res(attn.reshape(bsz * seq, d_mix), bf(w_o[i]), x2)
        x2 = _ffn(x2, g_ffn2[l], bf(w_ffn2_gate[l]), bf(w_ffn2_up[l]), bf(w_ffn2_down[l]),
                  g_final, final_norm=(l == depth - 1))
    return x2.reshape(bsz, seq, d)
```

```python
import functools

import jax
import jax.numpy as jnp
import numpy as np
from jax import lax
from jax.experimental import pallas as pl
from jax.experimental.pallas import tpu as pltpu

EPS = 1e-6
LANES = 128
CONV_HALO = 32
GROUPS_B = 4
CHUNK = 128
HEAD_DIM = 64
VMEM_LIMIT = 56 * 1024 * 1024

F32 = jnp.float32
BF16 = jnp.bfloat16


def _resident(shape):
    zeros = (0,) * len(shape)
    return pl.BlockSpec(shape, lambda *_: zeros, pipeline_mode=pl.Buffered(1))


def _rms(x, g):
    ms = jnp.mean(x * x, axis=-1, keepdims=True)
    return x * lax.rsqrt(ms + EPS) * g


def _ffn_kernel(x_ref, g_ref, wg_ref, wu_ref, wd_ref, gf_ref, *rest, tf, final_norm, mixer_proj):
    if mixer_proj:
        mix_ref, wo_ref, o_ref, act_ref = rest
        x = x_ref[...] + jnp.dot(mix_ref[...], wo_ref[...], preferred_element_type=F32)
    else:
        o_ref, act_ref = rest
        x = x_ref[...]
    h = _rms(x, g_ref[...]).astype(BF16)
    d_ff = wg_ref.shape[1]
    for j in range(d_ff // tf):
        cols = slice(j * tf, (j + 1) * tf)
        gate = jnp.dot(h, wg_ref[:, cols], preferred_element_type=F32)
        up = jnp.dot(h, wu_ref[:, cols], preferred_element_type=F32)
        act_ref[:, cols] = (gate * jax.nn.sigmoid(gate) * up).astype(BF16)
    y = jnp.dot(act_ref[...], wd_ref[...], preferred_element_type=F32)
    out = x + 0.5 * y
    if final_norm:
        out = _rms(out, gf_ref[...])
    o_ref[...] = out


def _ffn(x2, g, wg, wu, wd, g_final, *, final_norm, mixer=None, tm=512, tf=256):
    m, d = x2.shape
    d_ff = wg.shape[1]
    kern = functools.partial(_ffn_kernel, tf=tf, final_norm=final_norm,
                             mixer_proj=mixer is not None)
    in_specs = [
        pl.BlockSpec((tm, d), lambda i: (i, 0)),
        _resident((1, d)),
        _resident((d, d_ff)),
        _resident((d, d_ff)),
        _resident((d_ff, d)),
        _resident((1, d)),
    ]
    args = [x2, g.reshape(1, d), wg, wu, wd, g_final.reshape(1, d)]
    if mixer is not None:
        mix, w_o = mixer
        in_specs += [pl.BlockSpec((tm, mix.shape[1]), lambda i: (i, 0)), _resident(w_o.shape)]
        args += [mix, w_o]
    return pl.pallas_call(
        kern,
        out_shape=jax.ShapeDtypeStruct((m, d), F32),
        grid=(m // tm,),
        in_specs=in_specs,
        out_specs=pl.BlockSpec((tm, d), lambda i: (i, 0)),
        scratch_shapes=[pltpu.VMEM((tm, d_ff), BF16)],
        compiler_params=pltpu.CompilerParams(
            dimension_semantics=("arbitrary",), vmem_limit_bytes=VMEM_LIMIT),
        name="ffn",
    )(*args)


def _ab_in_kernel(x_ref, g_ref, w_ref, lvg_ref, lvb_ref, a_ref, u_ref, v_ref):
    h = _rms(x_ref[...], g_ref[...]).astype(BF16)
    z = jnp.dot(h, w_ref[...], preferred_element_type=F32)
    c = a_ref.shape[1]
    a_ref[...] = z[:, :c] * jax.nn.sigmoid(z[:, c:2 * c])
    u_ref[...] = z[:, 2 * c:3 * c]
    lvg = lvg_ref[...]
    lvb = lvb_ref[...]
    for grp in range(GROUPS_B):
        lo = 3 * c + grp * LANES
        vg = z[:, lo:lo + LANES]
        mu = jnp.mean(vg, axis=-1, keepdims=True)
        xc = vg - mu
        var = jnp.mean(xc * xc, axis=-1, keepdims=True)
        sl = slice(grp * LANES, (grp + 1) * LANES)
        v_ref[:, sl] = (xc * lax.rsqrt(var + EPS) * lvg[:, sl] + lvb[:, sl]).astype(BF16)


def _ab_in(x2, g, w_in, ln_v_g, ln_v_b, *, tm=512):
    m, d = x2.shape
    n = w_in.shape[1]
    c = n // 4
    return pl.pallas_call(
        _ab_in_kernel,
        out_shape=(jax.ShapeDtypeStruct((m, c), F32),
                   jax.ShapeDtypeStruct((m, c), F32),
                   jax.ShapeDtypeStruct((m, c), BF16)),
        grid=(m // tm,),
        in_specs=[
            pl.BlockSpec((tm, d), lambda i: (i, 0)),
            _resident((1, d)),
            _resident((d, n)),
            _resident((1, c)),
            _resident((1, c)),
        ],
        out_specs=(pl.BlockSpec((tm, c), lambda i: (i, 0)),
                   pl.BlockSpec((tm, c), lambda i: (i, 0)),
                   pl.BlockSpec((tm, c), lambda i: (i, 0))),
        compiler_params=pltpu.CompilerParams(
            dimension_semantics=("arbitrary",), vmem_limit_bytes=VMEM_LIMIT),
        name="ab_in",
    )(x2, g.reshape(1, d), w_in, ln_v_g.reshape(1, c), ln_v_b.reshape(1, c))


def _ab_mix_kernel(ac_ref, ap_ref, u_ref, v_ref, x_ref, cw_ref, cb_ref, lag_ref, lab_ref,
                   spw_ref, spb_ref, wo_ref, o_ref, abuf_ref, m_ref):
    ts, c = ac_ref.shape[1], ac_ref.shape[2]
    ktaps = cw_ref.shape[0]
    first = pl.program_id(1) == 0
    abuf_ref[0:CONV_HALO, :] = jnp.where(first, 0.0, ap_ref[0])
    abuf_ref[CONV_HALO:, :] = ac_ref[0]
    conv = jnp.zeros((ts, c), F32) + cb_ref[...]
    for k in range(ktaps):
        off = CONV_HALO - (ktaps - 1) + k
        conv = conv + cw_ref[k:k + 1, :] * abuf_ref[off:off + ts, :]
    mu = jnp.mean(conv, axis=-1, keepdims=True)
    xc = conv - mu
    var = jnp.mean(xc * xc, axis=-1, keepdims=True)
    a = xc * lax.rsqrt(var + EPS) * lag_ref[...] + lab_ref[...]
    m_ref[:, 0:c] = (a * jax.nn.sigmoid(a)).astype(BF16)

    row = lax.broadcasted_iota(jnp.int32, (CHUNK, CHUNK), 0)
    col = lax.broadcasted_iota(jnp.int32, (CHUNK, CHUNK), 1)
    causal = row >= col
    for grp in range(GROUPS_B):
        w = jnp.where(causal, spw_ref[grp], 0.0).astype(BF16)
        bias = spb_ref[:, grp:grp + 1]
        sl = slice(grp * LANES, (grp + 1) * LANES)
        for ch in range(ts // CHUNK):
            rows = slice(ch * CHUNK, (ch + 1) * CHUNK)
            mixed = jnp.dot(w, v_ref[0, rows, sl], preferred_element_type=F32) + bias
            m_ref[rows, c + grp * LANES:c + (grp + 1) * LANES] = (
                u_ref[0, rows, sl] * mixed).astype(BF16)

    y = jnp.dot(m_ref[...], wo_ref[...], preferred_element_type=F32)
    o_ref[0] = x_ref[0] + y


def _ab_mix(a, u, v, x3, conv_w, conv_b, ln_a_g, ln_a_b, sp_w, sp_b, w_out, *, ts=512):
    b, s, d = x3.shape
    c = a.shape[-1]
    ktaps = conv_w.shape[0]
    assert ktaps - 1 <= CONV_HALO and ts % CONV_HALO == 0 and ts % CHUNK == 0
    halo_blocks = ts // CONV_HALO
    return pl.pallas_call(
        _ab_mix_kernel,
        out_shape=jax.ShapeDtypeStruct((b, s, d), F32),
        grid=(b, s // ts),
        in_specs=[
            pl.BlockSpec((1, ts, c), lambda bi, i: (bi, i, 0)),
            pl.BlockSpec((1, CONV_HALO, c),
                         lambda bi, i: (bi, jnp.maximum(i * halo_blocks - 1, 0), 0)),
            pl.BlockSpec((1, ts, c), lambda bi, i: (bi, i, 0)),
            pl.BlockSpec((1, ts, c), lambda bi, i: (bi, i, 0)),
            pl.BlockSpec((1, ts, d), lambda bi, i: (bi, i, 0)),
            _resident((ktaps, c)),
            _resident((1, c)),
            _resident((1, c)),
            _resident((1, c)),
            _resident((GROUPS_B, CHUNK, CHUNK)),
            _resident((CHUNK, GROUPS_B)),
            _resident((2 * c, d)),
        ],
        out_specs=pl.BlockSpec((1, ts, d), lambda bi, i: (bi, i, 0)),
        scratch_shapes=[pltpu.VMEM((ts + CONV_HALO, c), F32),
                        pltpu.VMEM((ts, 2 * c), BF16)],
        compiler_params=pltpu.CompilerParams(
            dimension_semantics=("arbitrary", "arbitrary"), vmem_limit_bytes=VMEM_LIMIT),
        name="ab_mix",
    )(a.reshape(b, s, c), a.reshape(b, s, c), u.reshape(b, s, c), v.reshape(b, s, c), x3,
      conv_w, conv_b.reshape(1, c), ln_a_g.reshape(1, c), ln_a_b.reshape(1, c),
      sp_w, sp_b.T, w_out)


def _qkv_kernel(x_ref, g_ref, w_ref, o_ref, *, d_mix):
    h = _rms(x_ref[...], g_ref[...]).astype(BF16)
    z = jnp.dot(h, w_ref[...], preferred_element_type=F32)
    o_ref[:, :d_mix] = (z[:, :d_mix] * (HEAD_DIM ** -0.5)).astype(BF16)
    o_ref[:, d_mix:] = z[:, d_mix:].astype(BF16)


def _qkv(x2, g, w_qkv, *, tm=512):
    m, d = x2.shape
    n = w_qkv.shape[1]
    return pl.pallas_call(
        functools.partial(_qkv_kernel, d_mix=n // 3),
        out_shape=jax.ShapeDtypeStruct((m, n), BF16),
        grid=(m // tm,),
        in_specs=[
            pl.BlockSpec((tm, d), lambda i: (i, 0)),
            _resident((1, d)),
            _resident((d, n)),
        ],
        out_specs=pl.BlockSpec((tm, n), lambda i: (i, 0)),
        compiler_params=pltpu.CompilerParams(
            dimension_semantics=("arbitrary",), vmem_limit_bytes=VMEM_LIMIT),
        name="qkv",
    )(x2, g.reshape(1, d), w_qkv)


ATTN_BLOCK = 256
ATTN_ROW_CHUNK = 128
ATTN_UNROLL = 6
ATTN_LAG = 2
ATTN_ACC_SLOTS = 4
MASK_BIAS = -1e30
LOG2E = 1.4426950408889634


def _pair_table(nq):
    rows = [(i, j, int(j == i), int(j == 0), i % ATTN_ACC_SLOTS)
            for i in range(nq) for j in range(i, -1, -1)]
    n = len(rows) + ATTN_LAG
    n += (-n) % ATTN_UNROLL
    rows += [(0, 0, 1, 0, nq % ATTN_ACC_SLOTS)] * (n - len(rows))
    return np.asarray(rows, dtype=np.int32).T.copy()


def _attn_kernel(tab_ref, q_ref, k_ref, v_ref, bias_ref, o_ref,
                 qm_ref, sp_ref, zb_ref, a_ref, acc_ref, rem_ref):
    tq = tk = ATTN_BLOCK
    rc = ATTN_ROW_CHUNK
    seq = q_ref.shape[1]
    nsteps = tab_ref.shape[1]
    nchunk = 2 * tq // rc
    lane_q = lax.broadcasted_iota(jnp.int32, (seq, LANES), 1)
    q_all = q_ref[0]
    zero = jnp.zeros_like(q_all)
    qm_ref[0] = jnp.where(lane_q < HEAD_DIM, q_all, zero)
    qm_ref[1] = jnp.where(lane_q >= HEAD_DIM, q_all, zero)
    kr = lax.broadcasted_iota(jnp.int32, (tk, tk), 0)
    kc = lax.broadcasted_iota(jnp.int32, (tk, tk), 1)
    tri = jnp.where(kr >= kc, 1.0, 0.0).astype(BF16)
    for ref in (sp_ref, zb_ref, a_ref, acc_ref, rem_ref):
        ref[...] = jnp.zeros_like(ref)
    lane = lax.broadcasted_iota(jnp.int32, (tq, LANES), 1)

    def matmuls(t, c, r):
        rows = slice(c * rc, (c + 1) * rc)
        head, off = divmod(c * rc, tq)
        q_rows = pl.ds(pl.multiple_of(tab_ref[0, t] * tq, tq) + off, rc)
        k_sb = k_ref[0, pl.ds(pl.multiple_of(tab_ref[1, t] * tk, tk), tk), :]
        z = lax.dot_general(qm_ref[head, q_rows, :], k_sb, (((1,), (1,)), ((), ())),
                            preferred_element_type=F32)
        sums = jnp.dot(sp_ref[r, rows, :], tri, preferred_element_type=F32)
        tm2 = jnp.maximum(t - 2, 0)
        v_sb = v_ref[0, pl.ds(pl.multiple_of(tab_ref[1, tm2] * tk, tk), tk), :]
        pv = jnp.dot(a_ref[r, rows, :], v_sb, preferred_element_type=F32)
        return z, sums, pv

    def elementwise(t, c, w, r, z, sums, pv):
        rows = slice(c * rc, (c + 1) * rc)
        tm1 = jnp.maximum(t - 1, 0)
        tm2 = jnp.maximum(t - 2, 0)
        zb = z + bias_ref[tab_ref[2, t], rows, :]
        e = jnp.exp2(jnp.abs(zb) * (-LOG2E))
        sp_ref[w, rows, :] = (jnp.maximum(zb, 0.0) + jnp.log(1.0 + e)).astype(BF16)
        zb_ref[w, rows, :] = zb
        rem = jnp.where(tab_ref[2, tm1] == 1, 0.0, rem_ref[rows, :])
        a = jnp.exp(zb_ref[r, rows, :] - sums - jnp.concatenate([rem, rem], axis=1))
        a_ref[w, rows, :] = a.astype(BF16)
        rem_ref[rows, :] = rem + jnp.broadcast_to(sums[:, 0:1], (rc, LANES))
        slot = tab_ref[4, tm2]
        acc_ref[slot, rows, :] = jnp.where(tab_ref[2, tm2] == 1, 0.0, acc_ref[slot, rows, :]) + pv

    def body(tt, carry):
        pending = None
        for u in range(ATTN_UNROLL):
            for c in range(nchunk):
                t = ATTN_UNROLL * tt + u
                w, r = u % 2, (u + 1) % 2
                vals = matmuls(t, c, r)
                if pending is not None:
                    elementwise(*pending)
                pending = (t, c, w, r, *vals)
        elementwise(*pending)
        for u in range(ATTN_UNROLL):
            tm2 = jnp.maximum(ATTN_UNROLL * tt + u - 2, 0)

            @pl.when(tab_ref[3, tm2] == 1)
            def _():
                slot = tab_ref[4, tm2]
                rows = pl.ds(pl.multiple_of(tab_ref[0, tm2] * tq, tq), tq)
                o_ref[0, rows, :] = jnp.where(lane < HEAD_DIM, acc_ref[slot, 0:tq, :],
                                              acc_ref[slot, tq:, :]).astype(o_ref.dtype)
        return carry

    lax.fori_loop(0, nsteps // ATTN_UNROLL, body, 0)


def _attention(qkv3, d_mix):
    b, s, _ = qkv3.shape
    tq = ATTN_BLOCK
    pairs = d_mix // LANES
    assert s % tq == 0 and d_mix % LANES == 0 and ATTN_UNROLL % 2 == 0
    assert sum(i + 1 for i in range(1, ATTN_ACC_SLOTS)) >= ATTN_UNROLL
    tab = jnp.asarray(_pair_table(s // tq))
    row = lax.broadcasted_iota(jnp.int32, (2 * tq, tq), 0) % tq
    col = lax.broadcasted_iota(jnp.int32, (2 * tq, tq), 1)
    bias = jnp.stack([jnp.zeros((2 * tq, tq), F32),
                      jnp.where(col < row, 0.0, MASK_BIAS).astype(F32)])
    grid_spec = pltpu.PrefetchScalarGridSpec(
        num_scalar_prefetch=1,
        grid=(b, pairs),
        in_specs=[
            pl.BlockSpec((1, s, LANES), lambda bi, p, tab: (bi, 0, p)),
            pl.BlockSpec((1, s, LANES), lambda bi, p, tab: (bi, 0, pairs + p)),
            pl.BlockSpec((1, s, LANES), lambda bi, p, tab: (bi, 0, 2 * pairs + p)),
            pl.BlockSpec((2, 2 * tq, tq), lambda bi, p, tab: (0, 0, 0)),
        ],
        out_specs=pl.BlockSpec((1, s, LANES), lambda bi, p, tab: (bi, 0, p)),
        scratch_shapes=[
            pltpu.VMEM((2, s, LANES), BF16),
            pltpu.VMEM((2, 2 * tq, tq), BF16),
            pltpu.VMEM((2, 2 * tq, tq), F32),
            pltpu.VMEM((2, 2 * tq, tq), BF16),
            pltpu.VMEM((ATTN_ACC_SLOTS, 2 * tq, LANES), F32),
            pltpu.VMEM((2 * tq, LANES), F32),
        ],
    )
    return pl.pallas_call(
        _attn_kernel,
        out_shape=jax.ShapeDtypeStruct((b, s, d_mix), BF16),
        grid_spec=grid_spec,
        compiler_params=pltpu.CompilerParams(
            dimension_semantics=("arbitrary", "arbitrary"), vmem_limit_bytes=VMEM_LIMIT),
        name="sb_attn",
    )(tab, qkv3, qkv3, qkv3, bias)


def kernel(x, g_ffn1, w_ffn1_gate, w_ffn1_up, w_ffn1_down, g_mix, w_in_ab, conv_w, conv_b, ln_a_g, ln_a_b, ln_v_g, ln_v_b, sp_w, sp_b, w_out_ab, w_qkv, w_o, g_ffn2, w_ffn2_gate, w_ffn2_up, w_ffn2_down, g_final):
    bsz, seq, d = x.shape
    depth = g_ffn1.shape[0]
    x2 = x.reshape(bsz * seq, d)
    bf = lambda w: w.astype(BF16)
    for l in range(depth):
        x2 = _ffn(x2, g_ffn1[l], bf(w_ffn1_gate[l]), bf(w_ffn1_up[l]), bf(w_ffn1_down[l]),
                  g_final, final_norm=False)
        i = l // 2
        mixer = None
        if l % 2 == 0:
            a, u, v = _ab_in(x2, g_mix[l], bf(w_in_ab[i]), ln_v_g[i], ln_v_b[i])
            x3 = _ab_mix(a, u, v, x2.reshape(bsz, seq, d), conv_w[i], conv_b[i], ln_a_g[i],
                         ln_a_b[i], sp_w[i], sp_b[i], bf(w_out_ab[i]))
            x2 = x3.reshape(bsz * seq, d)
        else:
            d_mix = w_qkv.shape[2] // 3
            qkv = _qkv(x2, g_mix[l], bf(w_qkv[i]))
            attn = _attention(qkv.reshape(bsz, seq, 3 * d_mix), d_mix)
            mixer = (attn.reshape(bsz * seq, d_mix), bf(w_o[i]))
        x2 = _ffn(x2, g_ffn2[l], bf(w_ffn2_gate[l]), bf(w_ffn2_up[l]), bf(w_ffn2_down[l]),
                  g_final, final_norm=(l == depth - 1), mixer=mixer)
    return x2.reshape(bsz, seq, d)
```

```python
import functools

import jax
import jax.numpy as jnp
import numpy as np
from jax import lax
from jax.experimental import pallas as pl
from jax.experimental.pallas import tpu as pltpu

EPS = 1e-6
LANES = 128
SUBLANES = 8
CONV_HALO = 32
GROUPS_B = 4
CHUNK = 128
HEAD_DIM = 64
VMEM_LIMIT = 56 * 1024 * 1024

F32 = jnp.float32
BF16 = jnp.bfloat16


def _resident(shape):
    zeros = (0,) * len(shape)
    return pl.BlockSpec(shape, lambda *_: zeros, pipeline_mode=pl.Buffered(1))


def _rms(x, g):
    ms = jnp.mean(x * x, axis=-1, keepdims=True)
    return x * lax.rsqrt(ms + EPS) * g


def _ffn_kernel(x_ref, g_ref, wg_ref, wu_ref, wd_ref, gf_ref, *rest, tf, final_norm, mixer_proj):
    if mixer_proj:
        mix_ref, wo_ref, o_ref, act_ref = rest
        x = x_ref[...] + jnp.dot(mix_ref[...], wo_ref[...], preferred_element_type=F32)
    else:
        o_ref, act_ref = rest
        x = x_ref[...]
    h = _rms(x, g_ref[...]).astype(BF16)
    d_ff = wg_ref.shape[1]
    for j in range(d_ff // tf):
        cols = slice(j * tf, (j + 1) * tf)
        gate = jnp.dot(h, wg_ref[:, cols], preferred_element_type=F32)
        up = jnp.dot(h, wu_ref[:, cols], preferred_element_type=F32)
        act_ref[:, cols] = (gate * jax.nn.sigmoid(gate) * up).astype(BF16)
    y = jnp.dot(act_ref[...], wd_ref[...], preferred_element_type=F32)
    out = x + 0.5 * y
    if final_norm:
        out = _rms(out, gf_ref[...])
    o_ref[...] = out


def _ffn(x2, g, wg, wu, wd, g_final, *, final_norm, mixer=None, tm=1024, tf=256):
    m, d = x2.shape
    d_ff = wg.shape[1]
    kern = functools.partial(_ffn_kernel, tf=tf, final_norm=final_norm,
                             mixer_proj=mixer is not None)
    in_specs = [
        pl.BlockSpec((tm, d), lambda i: (i, 0)),
        _resident((1, d)),
        _resident((d, d_ff)),
        _resident((d, d_ff)),
        _resident((d_ff, d)),
        _resident((1, d)),
    ]
    args = [x2, g.reshape(1, d), wg, wu, wd, g_final.reshape(1, d)]
    if mixer is not None:
        mix, w_o = mixer
        in_specs += [pl.BlockSpec((tm, mix.shape[1]), lambda i: (i, 0)), _resident(w_o.shape)]
        args += [mix, w_o]
    return pl.pallas_call(
        kern,
        out_shape=jax.ShapeDtypeStruct((m, d), F32),
        grid=(m // tm,),
        in_specs=in_specs,
        out_specs=pl.BlockSpec((tm, d), lambda i: (i, 0)),
        scratch_shapes=[pltpu.VMEM((tm, d_ff), BF16)],
        compiler_params=pltpu.CompilerParams(
            dimension_semantics=("arbitrary",), vmem_limit_bytes=VMEM_LIMIT),
        name="ffn",
    )(*args)


def _ab_in_kernel(x_ref, g_ref, w_ref, lvg_ref, lvb_ref, a_ref, u_ref, v_ref):
    h = _rms(x_ref[...], g_ref[...]).astype(BF16)
    z = jnp.dot(h, w_ref[...], preferred_element_type=F32)
    c = a_ref.shape[1]
    a_ref[...] = z[:, :c] * jax.nn.sigmoid(z[:, c:2 * c])
    u_ref[...] = z[:, 2 * c:3 * c]
    lvg = lvg_ref[...]
    lvb = lvb_ref[...]
    for grp in range(GROUPS_B):
        lo = 3 * c + grp * LANES
        vg = z[:, lo:lo + LANES]
        mu = jnp.mean(vg, axis=-1, keepdims=True)
        xc = vg - mu
        var = jnp.mean(xc * xc, axis=-1, keepdims=True)
        sl = slice(grp * LANES, (grp + 1) * LANES)
        v_ref[:, sl] = (xc * lax.rsqrt(var + EPS) * lvg[:, sl] + lvb[:, sl]).astype(BF16)


def _ab_in(x2, g, w_in, ln_v_g, ln_v_b, *, tm=1024):
    m, d = x2.shape
    n = w_in.shape[1]
    c = n // 4
    return pl.pallas_call(
        _ab_in_kernel,
        out_shape=(jax.ShapeDtypeStruct((m, c), F32),
                   jax.ShapeDtypeStruct((m, c), F32),
                   jax.ShapeDtypeStruct((m, c), BF16)),
        grid=(m // tm,),
        in_specs=[
            pl.BlockSpec((tm, d), lambda i: (i, 0)),
            _resident((1, d)),
            _resident((d, n)),
            _resident((1, c)),
            _resident((1, c)),
        ],
        out_specs=(pl.BlockSpec((tm, c), lambda i: (i, 0)),
                   pl.BlockSpec((tm, c), lambda i: (i, 0)),
                   pl.BlockSpec((tm, c), lambda i: (i, 0))),
        compiler_params=pltpu.CompilerParams(
            dimension_semantics=("arbitrary",), vmem_limit_bytes=VMEM_LIMIT),
        name="ab_in",
    )(x2, g.reshape(1, d), w_in, ln_v_g.reshape(1, c), ln_v_b.reshape(1, c))


def _ab_mix_kernel(ac_ref, ap_ref, u_ref, v_ref, x_ref, cw_ref, cb_ref, lag_ref, lab_ref,
                   spw_ref, spb_ref, wo_ref, o_ref, abuf_ref, shift_ref, m_ref):
    ts, c = ac_ref.shape[1], ac_ref.shape[2]
    ktaps = cw_ref.shape[0]
    first = pl.program_id(1) == 0
    abuf_ref[0:CONV_HALO, :] = jnp.where(first, 0.0, ap_ref[0])
    abuf_ref[CONV_HALO:, :] = ac_ref[0]
    base = CONV_HALO - (ktaps - 1)
    span = shift_ref.shape[0]
    conv = jnp.zeros((ts, c), F32) + cb_ref[...]
    for res in range(SUBLANES):
        offs = [o for o in range(base, base + ktaps) if o % SUBLANES == res]
        if not offs:
            continue
        if res:
            shift_ref[...] = abuf_ref[res:res + span, :]
        for o in offs:
            al = o - res
            src = shift_ref[al:al + ts, :] if res else abuf_ref[al:al + ts, :]
            conv = conv + cw_ref[o - base:o - base + 1, :] * src
    mu = jnp.mean(conv, axis=-1, keepdims=True)
    xc = conv - mu
    var = jnp.mean(xc * xc, axis=-1, keepdims=True)
    a = xc * lax.rsqrt(var + EPS) * lag_ref[...] + lab_ref[...]
    m_ref[:, 0:c] = (a * jax.nn.sigmoid(a)).astype(BF16)

    row = lax.broadcasted_iota(jnp.int32, (CHUNK, CHUNK), 0)
    col = lax.broadcasted_iota(jnp.int32, (CHUNK, CHUNK), 1)
    causal = row >= col
    for grp in range(GROUPS_B):
        w = jnp.where(causal, spw_ref[grp], 0.0).astype(BF16)
        bias = spb_ref[:, grp:grp + 1]
        sl = slice(grp * LANES, (grp + 1) * LANES)
        for ch in range(ts // CHUNK):
            rows = slice(ch * CHUNK, (ch + 1) * CHUNK)
            mixed = jnp.dot(w, v_ref[0, rows, sl], preferred_element_type=F32) + bias
            m_ref[rows, c + grp * LANES:c + (grp + 1) * LANES] = (
                u_ref[0, rows, sl] * mixed).astype(BF16)

    y = jnp.dot(m_ref[...], wo_ref[...], preferred_element_type=F32)
    o_ref[0] = x_ref[0] + y


def _ab_mix(a, u, v, x3, conv_w, conv_b, ln_a_g, ln_a_b, sp_w, sp_b, w_out, *, ts=512):
    b, s, d = x3.shape
    c = a.shape[-1]
    ktaps = conv_w.shape[0]
    assert ktaps - 1 <= CONV_HALO and ts % CONV_HALO == 0 and ts % CHUNK == 0
    halo_blocks = ts // CONV_HALO
    return pl.pallas_call(
        _ab_mix_kernel,
        out_shape=jax.ShapeDtypeStruct((b, s, d), F32),
        grid=(b, s // ts),
        in_specs=[
            pl.BlockSpec((1, ts, c), lambda bi, i: (bi, i, 0)),
            pl.BlockSpec((1, CONV_HALO, c),
                         lambda bi, i: (bi, jnp.maximum(i * halo_blocks - 1, 0), 0)),
            pl.BlockSpec((1, ts, c), lambda bi, i: (bi, i, 0)),
            pl.BlockSpec((1, ts, c), lambda bi, i: (bi, i, 0)),
            pl.BlockSpec((1, ts, d), lambda bi, i: (bi, i, 0)),
            _resident((ktaps, c)),
            _resident((1, c)),
            _resident((1, c)),
            _resident((1, c)),
            _resident((GROUPS_B, CHUNK, CHUNK)),
            _resident((CHUNK, GROUPS_B)),
            _resident((2 * c, d)),
        ],
        out_specs=pl.BlockSpec((1, ts, d), lambda bi, i: (bi, i, 0)),
        scratch_shapes=[pltpu.VMEM((ts + CONV_HALO, c), F32),
                        pltpu.VMEM((ts + CONV_HALO - SUBLANES, c), F32),
                        pltpu.VMEM((ts, 2 * c), BF16)],
        compiler_params=pltpu.CompilerParams(
            dimension_semantics=("arbitrary", "arbitrary"), vmem_limit_bytes=VMEM_LIMIT),
        name="ab_mix",
    )(a.reshape(b, s, c), a.reshape(b, s, c), u.reshape(b, s, c), v.reshape(b, s, c), x3,
      conv_w, conv_b.reshape(1, c), ln_a_g.reshape(1, c), ln_a_b.reshape(1, c),
      sp_w, sp_b.T, w_out)


def _qkv_kernel(x_ref, g_ref, w_ref, o_ref, *, d_mix):
    h = _rms(x_ref[...], g_ref[...]).astype(BF16)
    z = jnp.dot(h, w_ref[...], preferred_element_type=F32)
    o_ref[:, :d_mix] = (z[:, :d_mix] * (HEAD_DIM ** -0.5)).astype(BF16)
    o_ref[:, d_mix:] = z[:, d_mix:].astype(BF16)


def _qkv(x2, g, w_qkv, *, tm=1024):
    m, d = x2.shape
    n = w_qkv.shape[1]
    return pl.pallas_call(
        functools.partial(_qkv_kernel, d_mix=n // 3),
        out_shape=jax.ShapeDtypeStruct((m, n), BF16),
        grid=(m // tm,),
        in_specs=[
            pl.BlockSpec((tm, d), lambda i: (i, 0)),
            _resident((1, d)),
            _resident((d, n)),
        ],
        out_specs=pl.BlockSpec((tm, n), lambda i: (i, 0)),
        compiler_params=pltpu.CompilerParams(
            dimension_semantics=("arbitrary",), vmem_limit_bytes=VMEM_LIMIT),
        name="qkv",
    )(x2, g.reshape(1, d), w_qkv)


ATTN_BLOCK = 256
ATTN_ROW_CHUNK = 128
ATTN_UNROLL = 6
ATTN_LAG = 2
ATTN_ACC_SLOTS = 4
MASK_BIAS = -1e30
LOG2E = 1.4426950408889634


def _pair_table(nq):
    rows = [(i, j, int(j == i), int(j == 0), i % ATTN_ACC_SLOTS)
            for i in range(nq) for j in range(i, -1, -1)]
    n = len(rows) + ATTN_LAG
    n += (-n) % ATTN_UNROLL
    rows += [(0, 0, 1, 0, nq % ATTN_ACC_SLOTS)] * (n - len(rows))
    return np.asarray(rows, dtype=np.int32).T.copy()


def _attn_kernel(tab_ref, q_ref, k_ref, v_ref, bias_ref, o_ref,
                 qm_ref, sp_ref, zb_ref, a_ref, acc_ref, rem_ref):
    tq = tk = ATTN_BLOCK
    rc = ATTN_ROW_CHUNK
    seq = q_ref.shape[1]
    nsteps = tab_ref.shape[1]
    nchunk = 2 * tq // rc
    lane_q = lax.broadcasted_iota(jnp.int32, (seq, LANES), 1)
    q_all = q_ref[0]
    zero = jnp.zeros_like(q_all)
    qm_ref[0] = jnp.where(lane_q < HEAD_DIM, q_all, zero)
    qm_ref[1] = jnp.where(lane_q >= HEAD_DIM, q_all, zero)
    kr = lax.broadcasted_iota(jnp.int32, (tk, tk), 0)
    kc = lax.broadcasted_iota(jnp.int32, (tk, tk), 1)
    tri = jnp.where(kr >= kc, 1.0, 0.0).astype(BF16)
    for ref in (sp_ref, zb_ref, a_ref, acc_ref, rem_ref):
        ref[...] = jnp.zeros_like(ref)
    lane = lax.broadcasted_iota(jnp.int32, (tq, LANES), 1)

    def matmuls(t, c, r):
        rows = slice(c * rc, (c + 1) * rc)
        head, off = divmod(c * rc, tq)
        q_rows = pl.ds(pl.multiple_of(tab_ref[0, t] * tq, tq) + off, rc)
        k_sb = k_ref[0, pl.ds(pl.multiple_of(tab_ref[1, t] * tk, tk), tk), :]
        z = lax.dot_general(qm_ref[head, q_rows, :], k_sb, (((1,), (1,)), ((), ())),
                            preferred_element_type=F32)
        sums = jnp.dot(sp_ref[r, rows, :], tri, preferred_element_type=F32)
        tm2 = jnp.maximum(t - 2, 0)
        v_sb = v_ref[0, pl.ds(pl.multiple_of(tab_ref[1, tm2] * tk, tk), tk), :]
        pv = jnp.dot(a_ref[r, rows, :], v_sb, preferred_element_type=F32)
        return z, sums, pv

    def elementwise(t, c, w, r, z, sums, pv):
        rows = slice(c * rc, (c + 1) * rc)
        tm1 = jnp.maximum(t - 1, 0)
        tm2 = jnp.maximum(t - 2, 0)
        zb = z + bias_ref[tab_ref[2, t], rows, :]
        zh = zb.astype(BF16)
        e = jnp.exp2(jnp.abs(zh) * jnp.asarray(-LOG2E, BF16))
        sp_ref[w, rows, :] = (jnp.maximum(zh, jnp.asarray(0.0, BF16))
                              + jnp.log(jnp.asarray(1.0, BF16) + e))
        zb_ref[w, rows, :] = zb
        rem = jnp.where(tab_ref[2, tm1] == 1, 0.0, rem_ref[rows, :])
        a = jnp.exp(zb_ref[r, rows, :] - sums - jnp.concatenate([rem, rem], axis=1))
        a_ref[w, rows, :] = a.astype(BF16)
        rem_ref[rows, :] = rem + jnp.broadcast_to(sums[:, 0:1], (rc, LANES))
        slot = tab_ref[4, tm2]
        acc_ref[slot, rows, :] = jnp.where(tab_ref[2, tm2] == 1, 0.0, acc_ref[slot, rows, :]) + pv

    def body(tt, carry):
        pending = None
        for u in range(ATTN_UNROLL):
            for c in range(nchunk):
                t = ATTN_UNROLL * tt + u
                w, r = u % 2, (u + 1) % 2
                vals = matmuls(t, c, r)
                if pending is not None:
                    elementwise(*pending)
                pending = (t, c, w, r, *vals)
        elementwise(*pending)
        for u in range(ATTN_UNROLL):
            tm2 = jnp.maximum(ATTN_UNROLL * tt + u - 2, 0)

            @pl.when(tab_ref[3, tm2] == 1)
            def _():
                slot = tab_ref[4, tm2]
                rows = pl.ds(pl.multiple_of(tab_ref[0, tm2] * tq, tq), tq)
                o_ref[0, rows, :] = jnp.where(lane < HEAD_DIM, acc_ref[slot, 0:tq, :],
                                              acc_ref[slot, tq:, :]).astype(o_ref.dtype)
        return carry

    lax.fori_loop(0, nsteps // ATTN_UNROLL, body, 0)


def _attention(qkv3, d_mix):
    b, s, _ = qkv3.shape
    tq = ATTN_BLOCK
    pairs = d_mix // LANES
    assert s % tq == 0 and d_mix % LANES == 0 and ATTN_UNROLL % 2 == 0
    assert sum(i + 1 for i in range(1, ATTN_ACC_SLOTS)) >= ATTN_UNROLL
    tab = jnp.asarray(_pair_table(s // tq))
    row = lax.broadcasted_iota(jnp.int32, (2 * tq, tq), 0) % tq
    col = lax.broadcasted_iota(jnp.int32, (2 * tq, tq), 1)
    bias = jnp.stack([jnp.zeros((2 * tq, tq), F32),
                      jnp.where(col < row, 0.0, MASK_BIAS).astype(F32)])
    grid_spec = pltpu.PrefetchScalarGridSpec(
        num_scalar_prefetch=1,
        grid=(b, pairs),
        in_specs=[
            pl.BlockSpec((1, s, LANES), lambda bi, p, tab: (bi, 0, p)),
            pl.BlockSpec((1, s, LANES), lambda bi, p, tab: (bi, 0, pairs + p)),
            pl.BlockSpec((1, s, LANES), lambda bi, p, tab: (bi, 0, 2 * pairs + p)),
            pl.BlockSpec((2, 2 * tq, tq), lambda bi, p, tab: (0, 0, 0)),
        ],
        out_specs=pl.BlockSpec((1, s, LANES), lambda bi, p, tab: (bi, 0, p)),
        scratch_shapes=[
            pltpu.VMEM((2, s, LANES), BF16),
            pltpu.VMEM((2, 2 * tq, tq), BF16),
            pltpu.VMEM((2, 2 * tq, tq), F32),
            pltpu.VMEM((2, 2 * tq, tq), BF16),
            pltpu.VMEM((ATTN_ACC_SLOTS, 2 * tq, LANES), F32),
            pltpu.VMEM((2 * tq, LANES), F32),
        ],
    )
    return pl.pallas_call(
        _attn_kernel,
        out_shape=jax.ShapeDtypeStruct((b, s, d_mix), BF16),
        grid_spec=grid_spec,
        compiler_params=pltpu.CompilerParams(
            dimension_semantics=("arbitrary", "arbitrary"), vmem_limit_bytes=VMEM_LIMIT),
        name="sb_attn",
    )(tab, qkv3, qkv3, qkv3, bias)


def kernel(x, g_ffn1, w_ffn1_gate, w_ffn1_up, w_ffn1_down, g_mix, w_in_ab, conv_w, conv_b, ln_a_g, ln_a_b, ln_v_g, ln_v_b, sp_w, sp_b, w_out_ab, w_qkv, w_o, g_ffn2, w_ffn2_gate, w_ffn2_up, w_ffn2_down, g_final):
    bsz, seq, d = x.shape
    depth = g_ffn1.shape[0]
    x2 = x.reshape(bsz * seq, d)
    bf = lambda w: w.astype(BF16)
    for l in range(depth):
        x2 = _ffn(x2, g_ffn1[l], bf(w_ffn1_gate[l]), bf(w_ffn1_up[l]), bf(w_ffn1_down[l]),
                  g_final, final_norm=False)
        i = l // 2
        mixer = None
        if l % 2 == 0:
            a, u, v = _ab_in(x2, g_mix[l], bf(w_in_ab[i]), ln_v_g[i], ln_v_b[i])
            x3 = _ab_mix(a, u, v, x2.reshape(bsz, seq, d), conv_w[i], conv_b[i], ln_a_g[i],
                         ln_a_b[i], sp_w[i], sp_b[i], bf(w_out_ab[i]))
            x2 = x3.reshape(bsz * seq, d)
        else:
            d_mix = w_qkv.shape[2] // 3
            qkv = _qkv(x2, g_mix[l], bf(w_qkv[i]))
            attn = _attention(qkv.reshape(bsz, seq, 3 * d_mix), d_mix)
            mixer = (attn.reshape(bsz * seq, d_mix), bf(w_o[i]))
        x2 = _ffn(x2, g_ffn2[l], bf(w_ffn2_gate[l]), bf(w_ffn2_up[l]), bf(w_ffn2_down[l]),
                  g_final, final_norm=(l == depth - 1), mixer=mixer)
    return x2.reshape(bsz, seq, d)
```

```python
import functools

import jax
import jax.numpy as jnp
import numpy as np
from jax import lax
from jax.experimental import pallas as pl
from jax.experimental.pallas import tpu as pltpu

EPS = 1e-6
LANES = 128
SUBLANES = 8
CONV_HALO = 32
GROUPS_B = 4
CHUNK = 128
HEAD_DIM = 64
VMEM_LIMIT = 56 * 1024 * 1024

F32 = jnp.float32
BF16 = jnp.bfloat16


def _resident(shape):
    zeros = (0,) * len(shape)
    return pl.BlockSpec(shape, lambda *_: zeros, pipeline_mode=pl.Buffered(1))


def _rms(x, g):
    ms = jnp.mean(x * x, axis=-1, keepdims=True)
    return x * lax.rsqrt(ms + EPS) * g


def _ffn_kernel(x_ref, g_ref, wg_ref, wu_ref, wd_ref, gf_ref, *rest, tf, final_norm, mixer_proj):
    if mixer_proj:
        mix_ref, wo_ref, o_ref, act_ref = rest
        x = x_ref[...] + jnp.dot(mix_ref[...], wo_ref[...], preferred_element_type=F32)
    else:
        o_ref, act_ref = rest
        x = x_ref[...]
    h = _rms(x, g_ref[...]).astype(BF16)
    d_ff = wg_ref.shape[1]
    for j in range(d_ff // tf):
        cols = slice(j * tf, (j + 1) * tf)
        gate = jnp.dot(h, wg_ref[:, cols], preferred_element_type=F32)
        up = jnp.dot(h, wu_ref[:, cols], preferred_element_type=F32)
        act_ref[:, cols] = (gate * jax.nn.sigmoid(gate) * up).astype(BF16)
    y = jnp.dot(act_ref[...], wd_ref[...], preferred_element_type=F32)
    out = x + 0.5 * y
    if final_norm:
        out = _rms(out, gf_ref[...])
    o_ref[...] = out


def _ffn(x2, g, wg, wu, wd, g_final, *, final_norm, mixer=None, tm=1024, tf=256):
    m, d = x2.shape
    d_ff = wg.shape[1]
    kern = functools.partial(_ffn_kernel, tf=tf, final_norm=final_norm,
                             mixer_proj=mixer is not None)
    in_specs = [
        pl.BlockSpec((tm, d), lambda i: (i, 0)),
        _resident((1, d)),
        _resident((d, d_ff)),
        _resident((d, d_ff)),
        _resident((d_ff, d)),
        _resident((1, d)),
    ]
    args = [x2, g.reshape(1, d), wg, wu, wd, g_final.reshape(1, d)]
    if mixer is not None:
        mix, w_o = mixer
        in_specs += [pl.BlockSpec((tm, mix.shape[1]), lambda i: (i, 0)), _resident(w_o.shape)]
        args += [mix, w_o]
    return pl.pallas_call(
        kern,
        out_shape=jax.ShapeDtypeStruct((m, d), F32),
        grid=(m // tm,),
        in_specs=in_specs,
        out_specs=pl.BlockSpec((tm, d), lambda i: (i, 0)),
        scratch_shapes=[pltpu.VMEM((tm, d_ff), BF16)],
        compiler_params=pltpu.CompilerParams(
            dimension_semantics=("arbitrary",), vmem_limit_bytes=VMEM_LIMIT),
        name="ffn",
    )(*args)


def _ab_in_kernel(x_ref, g_ref, w_ref, lvg_ref, lvb_ref, a_ref, u_ref, v_ref):
    h = _rms(x_ref[...], g_ref[...]).astype(BF16)
    z = jnp.dot(h, w_ref[...], preferred_element_type=F32)
    c = a_ref.shape[1]
    a_ref[...] = z[:, :c] * jax.nn.sigmoid(z[:, c:2 * c])
    u_ref[...] = z[:, 2 * c:3 * c]
    lvg = lvg_ref[...]
    lvb = lvb_ref[...]
    for grp in range(GROUPS_B):
        lo = 3 * c + grp * LANES
        vg = z[:, lo:lo + LANES]
        mu = jnp.mean(vg, axis=-1, keepdims=True)
        xc = vg - mu
        var = jnp.mean(xc * xc, axis=-1, keepdims=True)
        sl = slice(grp * LANES, (grp + 1) * LANES)
        v_ref[:, sl] = (xc * lax.rsqrt(var + EPS) * lvg[:, sl] + lvb[:, sl]).astype(BF16)


def _ab_in(x2, g, w_in, ln_v_g, ln_v_b, *, tm=1024):
    m, d = x2.shape
    n = w_in.shape[1]
    c = n // 4
    return pl.pallas_call(
        _ab_in_kernel,
        out_shape=(jax.ShapeDtypeStruct((m, c), F32),
                   jax.ShapeDtypeStruct((m, c), F32),
                   jax.ShapeDtypeStruct((m, c), BF16)),
        grid=(m // tm,),
        in_specs=[
            pl.BlockSpec((tm, d), lambda i: (i, 0)),
            _resident((1, d)),
            _resident((d, n)),
            _resident((1, c)),
            _resident((1, c)),
        ],
        out_specs=(pl.BlockSpec((tm, c), lambda i: (i, 0)),
                   pl.BlockSpec((tm, c), lambda i: (i, 0)),
                   pl.BlockSpec((tm, c), lambda i: (i, 0))),
        compiler_params=pltpu.CompilerParams(
            dimension_semantics=("arbitrary",), vmem_limit_bytes=VMEM_LIMIT),
        name="ab_in",
    )(x2, g.reshape(1, d), w_in, ln_v_g.reshape(1, c), ln_v_b.reshape(1, c))


def _ab_mix_kernel(ac_ref, ap_ref, u_ref, v_ref, x_ref, cw_ref, cb_ref, lag_ref, lab_ref,
                   spw_ref, spb_ref, wo_ref, o_ref, abuf_ref, shift_ref, m_ref):
    ts, c = ac_ref.shape[1], ac_ref.shape[2]
    ktaps = cw_ref.shape[0]
    first = pl.program_id(1) == 0
    abuf_ref[0:CONV_HALO, :] = jnp.where(first, 0.0, ap_ref[0])
    abuf_ref[CONV_HALO:, :] = ac_ref[0]
    base = CONV_HALO - (ktaps - 1)
    span = shift_ref.shape[0]
    conv = jnp.zeros((ts, c), F32) + cb_ref[...]
    for res in range(SUBLANES):
        offs = [o for o in range(base, base + ktaps) if o % SUBLANES == res]
        if not offs:
            continue
        if res:
            shift_ref[...] = abuf_ref[res:res + span, :]
        for o in offs:
            al = o - res
            src = shift_ref[al:al + ts, :] if res else abuf_ref[al:al + ts, :]
            conv = conv + cw_ref[o - base:o - base + 1, :] * src
    mu = jnp.mean(conv, axis=-1, keepdims=True)
    xc = conv - mu
    var = jnp.mean(xc * xc, axis=-1, keepdims=True)
    a = xc * lax.rsqrt(var + EPS) * lag_ref[...] + lab_ref[...]
    m_ref[:, 0:c] = (a * jax.nn.sigmoid(a)).astype(BF16)

    row = lax.broadcasted_iota(jnp.int32, (CHUNK, CHUNK), 0)
    col = lax.broadcasted_iota(jnp.int32, (CHUNK, CHUNK), 1)
    causal = row >= col
    for grp in range(GROUPS_B):
        w = jnp.where(causal, spw_ref[grp], 0.0).astype(BF16)
        bias = spb_ref[:, grp:grp + 1]
        sl = slice(grp * LANES, (grp + 1) * LANES)
        for ch in range(ts // CHUNK):
            rows = slice(ch * CHUNK, (ch + 1) * CHUNK)
            mixed = jnp.dot(w, v_ref[0, rows, sl], preferred_element_type=F32) + bias
            m_ref[rows, c + grp * LANES:c + (grp + 1) * LANES] = (
                u_ref[0, rows, sl] * mixed).astype(BF16)

    y = jnp.dot(m_ref[...], wo_ref[...], preferred_element_type=F32)
    o_ref[0] = x_ref[0] + y


def _ab_mix(a, u, v, x3, conv_w, conv_b, ln_a_g, ln_a_b, sp_w, sp_b, w_out, *, ts=512):
    b, s, d = x3.shape
    c = a.shape[-1]
    ktaps = conv_w.shape[0]
    assert ktaps - 1 <= CONV_HALO and ts % CONV_HALO == 0 and ts % CHUNK == 0
    halo_blocks = ts // CONV_HALO
    return pl.pallas_call(
        _ab_mix_kernel,
        out_shape=jax.ShapeDtypeStruct((b, s, d), F32),
        grid=(b, s // ts),
        in_specs=[
            pl.BlockSpec((1, ts, c), lambda bi, i: (bi, i, 0)),
            pl.BlockSpec((1, CONV_HALO, c),
                         lambda bi, i: (bi, jnp.maximum(i * halo_blocks - 1, 0), 0)),
            pl.BlockSpec((1, ts, c), lambda bi, i: (bi, i, 0)),
            pl.BlockSpec((1, ts, c), lambda bi, i: (bi, i, 0)),
            pl.BlockSpec((1, ts, d), lambda bi, i: (bi, i, 0)),
            _resident((ktaps, c)),
            _resident((1, c)),
            _resident((1, c)),
            _resident((1, c)),
            _resident((GROUPS_B, CHUNK, CHUNK)),
            _resident((CHUNK, GROUPS_B)),
            _resident((2 * c, d)),
        ],
        out_specs=pl.BlockSpec((1, ts, d), lambda bi, i: (bi, i, 0)),
        scratch_shapes=[pltpu.VMEM((ts + CONV_HALO, c), F32),
                        pltpu.VMEM((ts + CONV_HALO - SUBLANES, c), F32),
                        pltpu.VMEM((ts, 2 * c), BF16)],
        compiler_params=pltpu.CompilerParams(
            dimension_semantics=("arbitrary", "arbitrary"), vmem_limit_bytes=VMEM_LIMIT),
        name="ab_mix",
    )(a.reshape(b, s, c), a.reshape(b, s, c), u.reshape(b, s, c), v.reshape(b, s, c), x3,
      conv_w, conv_b.reshape(1, c), ln_a_g.reshape(1, c), ln_a_b.reshape(1, c),
      sp_w, sp_b.T, w_out)


def _qkv_kernel(x_ref, g_ref, w_ref, o_ref, *, d_mix):
    h = _rms(x_ref[...], g_ref[...]).astype(BF16)
    z = jnp.dot(h, w_ref[...], preferred_element_type=F32)
    o_ref[:, :d_mix] = (z[:, :d_mix] * (HEAD_DIM ** -0.5)).astype(BF16)
    o_ref[:, d_mix:] = z[:, d_mix:].astype(BF16)


def _qkv(x2, g, w_qkv, *, tm=1024):
    m, d = x2.shape
    n = w_qkv.shape[1]
    return pl.pallas_call(
        functools.partial(_qkv_kernel, d_mix=n // 3),
        out_shape=jax.ShapeDtypeStruct((m, n), BF16),
        grid=(m // tm,),
        in_specs=[
            pl.BlockSpec((tm, d), lambda i: (i, 0)),
            _resident((1, d)),
            _resident((d, n)),
        ],
        out_specs=pl.BlockSpec((tm, n), lambda i: (i, 0)),
        compiler_params=pltpu.CompilerParams(
            dimension_semantics=("arbitrary",), vmem_limit_bytes=VMEM_LIMIT),
        name="qkv",
    )(x2, g.reshape(1, d), w_qkv)


ATTN_BLOCK = 256
ATTN_ROW_CHUNK = 128
ATTN_UNROLL = 6
ATTN_LAG = 2
MASK_BIAS = -1e30
DEAD_REMAINDER = 120.0
LOG2E = 1.4426950408889634


def _wave_tables(nq):
    dummy = (0, 0, 0, nq)
    pairs = [dummy] * ATTN_LAG
    waves = []
    for d in range(nq):
        real = [(i, i - d, int(d == 0), i) for i in range(d, nq)]
        n = len(real) + ATTN_LAG
        n += (-n) % ATTN_UNROLL
        waves.append(((len(pairs) - ATTN_LAG) // ATTN_UNROLL, n // ATTN_UNROLL))
        pairs += real + [dummy] * (n - len(real))
    return (np.asarray(pairs, dtype=np.int32).T.copy(), np.asarray(waves, dtype=np.int32).T.copy())


def _attn_kernel(tab_ref, wave_ref, q_ref, k_ref, v_ref, bias_ref, o_ref,
                 qm_ref, sp_ref, ls_ref, a_ref, acc_ref, rem_ref):
    tq = tk = ATTN_BLOCK
    rc = ATTN_ROW_CHUNK
    seq = q_ref.shape[1]
    nq = seq // tq
    nchunk = 2 * tq // rc
    lane_q = lax.broadcasted_iota(jnp.int32, (seq, LANES), 1)
    q_all = q_ref[0]
    zero = jnp.zeros_like(q_all)
    qm_ref[0] = jnp.where(lane_q < HEAD_DIM, q_all, zero)
    qm_ref[1] = jnp.where(lane_q >= HEAD_DIM, q_all, zero)
    kr = lax.broadcasted_iota(jnp.int32, (tk, tk), 0)
    kc = lax.broadcasted_iota(jnp.int32, (tk, tk), 1)
    tri = jnp.where(kr > kc, 1.0, 0.0).astype(BF16)
    for ref in (sp_ref, ls_ref, a_ref, acc_ref, rem_ref):
        ref[...] = jnp.zeros_like(ref)

    def matmuls(t, c, r):
        rows = slice(c * rc, (c + 1) * rc)
        head, off = divmod(c * rc, tq)
        q_rows = pl.ds(pl.multiple_of(tab_ref[0, t + 2] * tq, tq) + off, rc)
        k_sb = k_ref[0, pl.ds(pl.multiple_of(tab_ref[1, t + 2] * tk, tk), tk), :]
        z = lax.dot_general(qm_ref[head, q_rows, :], k_sb, (((1,), (1,)), ((), ())),
                            preferred_element_type=F32)
        sums = jnp.dot(sp_ref[r, rows, :], tri, preferred_element_type=F32)
        v_sb = v_ref[0, pl.ds(pl.multiple_of(tab_ref[1, t] * tk, tk), tk), :]
        pv = jnp.dot(a_ref[r, rows, :], v_sb, preferred_element_type=F32)
        return z, sums, pv

    def elementwise(t, c, w, r, z, sums, pv):
        rows = slice(c * rc, (c + 1) * rc)
        zb = z + bias_ref[tab_ref[2, t + 2], rows, :]
        e = jnp.exp2(jnp.abs(zb) * (-LOG2E))
        softplus = jnp.maximum(zb, 0.0) + jnp.log(1.0 + e)
        sp_ref[w, rows, :] = softplus.astype(BF16)
        ls_ref[w, rows, :] = zb - softplus
        s1 = tab_ref[3, t + 1]
        rem = rem_ref[s1, rows, :]
        a = jnp.exp(ls_ref[r, rows, :] - sums - jnp.concatenate([rem, rem], axis=1))
        a_ref[w, rows, :] = a.astype(BF16)
        total = sums[:, 0:1] + sp_ref[r, rows, 0:1].astype(F32)
        rem_ref[s1, rows, :] = rem + jnp.broadcast_to(total, (rc, LANES))
        s2 = tab_ref[3, t]
        acc_ref[s2, rows, :] = acc_ref[s2, rows, :] + pv

    def body(bb, carry):
        pending = None
        for u in range(ATTN_UNROLL):
            for c in range(nchunk):
                t = ATTN_UNROLL * bb + u
                w, r = u % 2, (u + 1) % 2
                vals = matmuls(t, c, r)
                if pending is not None:
                    elementwise(*pending)
                pending = (t, c, w, r, *vals)
        elementwise(*pending)
        return carry

    def wave(state):
        d, _ = state
        first_body = wave_ref[0, d]
        lax.fori_loop(first_body, first_body + wave_ref[1, d], body, 0)

        def block_min(i, m):
            parts = [rem_ref[i, k * tq // 4:(k + 1) * tq // 4, :] for k in range(8)]
            while len(parts) > 1:
                parts = [jnp.minimum(a, b) for a, b in zip(parts[::2], parts[1::2])]
            part = parts[0]
            rows = [part[k * SUBLANES:(k + 1) * SUBLANES, :] for k in range(tq // 4 // SUBLANES)]
            while len(rows) > 1:
                rows = [jnp.minimum(a, b) for a, b in zip(rows[::2], rows[1::2])]
            return jnp.minimum(m, rows[0])

        m = lax.fori_loop(d + 1, nq, block_min, jnp.full((SUBLANES, LANES), jnp.inf, F32))
        return d + 1, jnp.min(m) > DEAD_REMAINDER

    lax.while_loop(lambda st: jnp.logical_and(st[0] < nq, jnp.logical_not(st[1])), wave,
                   (jnp.int32(0), jnp.bool_(False)))

    lane = lax.broadcasted_iota(jnp.int32, (tq, LANES), 1)

    def write_block(i, carry):
        rows = pl.ds(pl.multiple_of(i * tq, tq), tq)
        o_ref[0, rows, :] = jnp.where(lane < HEAD_DIM, acc_ref[i, 0:tq, :],
                                      acc_ref[i, tq:, :]).astype(o_ref.dtype)
        return carry

    lax.fori_loop(0, nq, write_block, 0)


def _attention(qkv3, d_mix):
    b, s, _ = qkv3.shape
    tq = ATTN_BLOCK
    nq = s // tq
    pairs = d_mix // LANES
    assert s % tq == 0 and d_mix % LANES == 0 and ATTN_UNROLL % 2 == 0
    tab, waves = _wave_tables(nq)
    row = lax.broadcasted_iota(jnp.int32, (2 * tq, tq), 0) % tq
    col = lax.broadcasted_iota(jnp.int32, (2 * tq, tq), 1)
    bias = jnp.stack([jnp.zeros((2 * tq, tq), F32),
                      jnp.where(col < row, 0.0, MASK_BIAS).astype(F32)])
    grid_spec = pltpu.PrefetchScalarGridSpec(
        num_scalar_prefetch=2,
        grid=(b, pairs),
        in_specs=[
            pl.BlockSpec((1, s, LANES), lambda bi, p, *_: (bi, 0, p)),
            pl.BlockSpec((1, s, LANES), lambda bi, p, *_: (bi, 0, pairs + p)),
            pl.BlockSpec((1, s, LANES), lambda bi, p, *_: (bi, 0, 2 * pairs + p)),
            pl.BlockSpec((2, 2 * tq, tq), lambda bi, p, *_: (0, 0, 0)),
        ],
        out_specs=pl.BlockSpec((1, s, LANES), lambda bi, p, *_: (bi, 0, p)),
        scratch_shapes=[
            pltpu.VMEM((2, s, LANES), BF16),
            pltpu.VMEM((2, 2 * tq, tq), BF16),
            pltpu.VMEM((2, 2 * tq, tq), F32),
            pltpu.VMEM((2, 2 * tq, tq), BF16),
            pltpu.VMEM((nq + 1, 2 * tq, LANES), F32),
            pltpu.VMEM((nq + 1, 2 * tq, LANES), F32),
        ],
    )
    return pl.pallas_call(
        _attn_kernel,
        out_shape=jax.ShapeDtypeStruct((b, s, d_mix), BF16),
        grid_spec=grid_spec,
        compiler_params=pltpu.CompilerParams(
            dimension_semantics=("arbitrary", "arbitrary"), vmem_limit_bytes=VMEM_LIMIT),
        name="sb_attn",
    )(jnp.asarray(tab), jnp.asarray(waves), qkv3, qkv3, qkv3, bias)


def kernel(x, g_ffn1, w_ffn1_gate, w_ffn1_up, w_ffn1_down, g_mix, w_in_ab, conv_w, conv_b, ln_a_g, ln_a_b, ln_v_g, ln_v_b, sp_w, sp_b, w_out_ab, w_qkv, w_o, g_ffn2, w_ffn2_gate, w_ffn2_up, w_ffn2_down, g_final):
    bsz, seq, d = x.shape
    depth = g_ffn1.shape[0]
    x2 = x.reshape(bsz * seq, d)
    bf = lambda w: w.astype(BF16)
    for l in range(depth):
        x2 = _ffn(x2, g_ffn1[l], bf(w_ffn1_gate[l]), bf(w_ffn1_up[l]), bf(w_ffn1_down[l]),
                  g_final, final_norm=False)
        i = l // 2
        mixer = None
        if l % 2 == 0:
            a, u, v = _ab_in(x2, g_mix[l], bf(w_in_ab[i]), ln_v_g[i], ln_v_b[i])
            x3 = _ab_mix(a, u, v, x2.reshape(bsz, seq, d), conv_w[i], conv_b[i], ln_a_g[i],
                         ln_a_b[i], sp_w[i], sp_b[i], bf(w_out_ab[i]))
            x2 = x3.reshape(bsz * seq, d)
        else:
            d_mix = w_qkv.shape[2] // 3
            qkv = _qkv(x2, g_mix[l], bf(w_qkv[i]))
            attn = _attention(qkv.reshape(bsz, seq, 3 * d_mix), d_mix)
            mixer = (attn.reshape(bsz * seq, d_mix), bf(w_o[i]))
        x2 = _ffn(x2, g_ffn2[l], bf(w_ffn2_gate[l]), bf(w_ffn2_up[l]), bf(w_ffn2_down[l]),
                  g_final, final_norm=(l == depth - 1), mixer=mixer)
    return x2.reshape(bsz, seq, d)
```

```python
import functools

import jax
import jax.numpy as jnp
import numpy as np
from jax import lax
from jax.experimental import pallas as pl
from jax.experimental.pallas import tpu as pltpu

EPS = 1e-6
LANES = 128
SUBLANES = 8
CONV_HALO = 32
GROUPS_B = 4
CHUNK = 128
HEAD_DIM = 64
VMEM_LIMIT = 56 * 1024 * 1024

F32 = jnp.float32
BF16 = jnp.bfloat16


def _resident(shape):
    zeros = (0,) * len(shape)
    return pl.BlockSpec(shape, lambda *_: zeros, pipeline_mode=pl.Buffered(1))


def _rms(x, g):
    ms = jnp.mean(x * x, axis=-1, keepdims=True)
    return x * lax.rsqrt(ms + EPS) * g


def _ffn_kernel(x_ref, g_ref, wg_ref, wu_ref, wd_ref, gf_ref, *rest, tf, final_norm, mixer_proj):
    if mixer_proj:
        mix_ref, wo_ref, o_ref, act_ref = rest
        x = x_ref[...] + jnp.dot(mix_ref[...], wo_ref[...], preferred_element_type=F32)
    else:
        o_ref, act_ref = rest
        x = x_ref[...]
    h = _rms(x, g_ref[...]).astype(BF16)
    d_ff = wg_ref.shape[1]
    for j in range(d_ff // tf):
        cols = slice(j * tf, (j + 1) * tf)
        gate = jnp.dot(h, wg_ref[:, cols], preferred_element_type=F32)
        up = jnp.dot(h, wu_ref[:, cols], preferred_element_type=F32)
        act_ref[:, cols] = (gate * jax.nn.sigmoid(gate) * up).astype(BF16)
    y = jnp.dot(act_ref[...], wd_ref[...], preferred_element_type=F32)
    out = x + 0.5 * y
    if final_norm:
        out = _rms(out, gf_ref[...])
    o_ref[...] = out


def _ffn(x2, g, wg, wu, wd, g_final, *, final_norm, mixer=None, tm=1024, tf=256):
    m, d = x2.shape
    d_ff = wg.shape[1]
    kern = functools.partial(_ffn_kernel, tf=tf, final_norm=final_norm,
                             mixer_proj=mixer is not None)
    in_specs = [
        pl.BlockSpec((tm, d), lambda i: (i, 0)),
        _resident((1, d)),
        _resident((d, d_ff)),
        _resident((d, d_ff)),
        _resident((d_ff, d)),
        _resident((1, d)),
    ]
    args = [x2, g.reshape(1, d), wg, wu, wd, g_final.reshape(1, d)]
    if mixer is not None:
        mix, w_o = mixer
        in_specs += [pl.BlockSpec((tm, mix.shape[1]), lambda i: (i, 0)), _resident(w_o.shape)]
        args += [mix, w_o]
    return pl.pallas_call(
        kern,
        out_shape=jax.ShapeDtypeStruct((m, d), F32),
        grid=(m // tm,),
        in_specs=in_specs,
        out_specs=pl.BlockSpec((tm, d), lambda i: (i, 0)),
        scratch_shapes=[pltpu.VMEM((tm, d_ff), BF16)],
        compiler_params=pltpu.CompilerParams(
            dimension_semantics=("arbitrary",), vmem_limit_bytes=VMEM_LIMIT),
        name="ffn",
    )(*args)


def _ab_kernel(x_ref, g_ref, wi_ref, lvg_ref, lvb_ref, cw_ref, cb_ref, lag_ref, lab_ref,
               spw_ref, spb_ref, wo_ref, o_ref, abuf_ref, shift_ref, u_ref, v_ref, m_ref):
    ts = x_ref.shape[1]
    c = cw_ref.shape[1]
    ktaps = cw_ref.shape[0]
    x = x_ref[0]
    h = _rms(x, g_ref[...]).astype(BF16)

    @pl.when(pl.program_id(1) == 0)
    def _():
        abuf_ref[0:CONV_HALO, :] = jnp.zeros((CONV_HALO, c), F32)

    za = jnp.dot(h, wi_ref[:, 0:2 * c], preferred_element_type=F32)
    abuf_ref[CONV_HALO:, :] = za[:, :c] * jax.nn.sigmoid(za[:, c:])
    zuv = jnp.dot(h, wi_ref[:, 2 * c:], preferred_element_type=F32)
    u_ref[...] = zuv[:, :c]
    lvg = lvg_ref[...]
    lvb = lvb_ref[...]
    for grp in range(GROUPS_B):
        sl = slice(grp * LANES, (grp + 1) * LANES)
        vg = zuv[:, c + grp * LANES:c + (grp + 1) * LANES]
        mu = jnp.mean(vg, axis=-1, keepdims=True)
        xc = vg - mu
        var = jnp.mean(xc * xc, axis=-1, keepdims=True)
        v_ref[:, sl] = (xc * lax.rsqrt(var + EPS) * lvg[:, sl] + lvb[:, sl]).astype(BF16)

    row = lax.broadcasted_iota(jnp.int32, (CHUNK, CHUNK), 0)
    col = lax.broadcasted_iota(jnp.int32, (CHUNK, CHUNK), 1)
    causal = row >= col
    for grp in range(GROUPS_B):
        w = jnp.where(causal, spw_ref[grp], 0.0).astype(BF16)
        bias = spb_ref[:, grp:grp + 1]
        sl = slice(grp * LANES, (grp + 1) * LANES)
        for ch in range(ts // CHUNK):
            rows = slice(ch * CHUNK, (ch + 1) * CHUNK)
            mixed = jnp.dot(w, v_ref[rows, sl], preferred_element_type=F32) + bias
            m_ref[rows, c + grp * LANES:c + (grp + 1) * LANES] = (
                u_ref[rows, sl] * mixed).astype(BF16)

    base = CONV_HALO - (ktaps - 1)
    span = shift_ref.shape[0]
    conv = jnp.zeros((ts, c), F32) + cb_ref[...]
    for res in range(SUBLANES):
        offs = [o for o in range(base, base + ktaps) if o % SUBLANES == res]
        if not offs:
            continue
        if res:
            shift_ref[...] = abuf_ref[res:res + span, :]
        for o in offs:
            al = o - res
            src = shift_ref[al:al + ts, :] if res else abuf_ref[al:al + ts, :]
            conv = conv + cw_ref[o - base:o - base + 1, :] * src
    abuf_ref[0:CONV_HALO, :] = abuf_ref[ts:ts + CONV_HALO, :]
    mu = jnp.mean(conv, axis=-1, keepdims=True)
    xc = conv - mu
    var = jnp.mean(xc * xc, axis=-1, keepdims=True)
    a = xc * lax.rsqrt(var + EPS) * lag_ref[...] + lab_ref[...]
    m_ref[:, 0:c] = (a * jax.nn.sigmoid(a)).astype(BF16)

    y = jnp.dot(m_ref[...], wo_ref[...], preferred_element_type=F32)
    o_ref[0] = x + y


def _ab(x3, g, w_in, ln_v_g, ln_v_b, conv_w, conv_b, ln_a_g, ln_a_b, sp_w, sp_b, w_out, *, ts=512):
    b, s, d = x3.shape
    c = w_in.shape[1] // 4
    ktaps = conv_w.shape[0]
    assert ktaps - 1 <= CONV_HALO and CONV_HALO % SUBLANES == 0 and s % ts == 0 and ts % CHUNK == 0
    assert c == GROUPS_B * LANES
    return pl.pallas_call(
        _ab_kernel,
        out_shape=jax.ShapeDtypeStruct((b, s, d), F32),
        grid=(b, s // ts),
        in_specs=[
            pl.BlockSpec((1, ts, d), lambda bi, i: (bi, i, 0)),
            _resident((1, d)),
            _resident((d, 4 * c)),
            _resident((1, c)),
            _resident((1, c)),
            _resident((ktaps, c)),
            _resident((1, c)),
            _resident((1, c)),
            _resident((1, c)),
            _resident((GROUPS_B, CHUNK, CHUNK)),
            _resident((CHUNK, GROUPS_B)),
            _resident((2 * c, d)),
        ],
        out_specs=pl.BlockSpec((1, ts, d), lambda bi, i: (bi, i, 0)),
        scratch_shapes=[pltpu.VMEM((ts + CONV_HALO, c), F32),
                        pltpu.VMEM((ts + CONV_HALO - SUBLANES, c), F32),
                        pltpu.VMEM((ts, c), F32),
                        pltpu.VMEM((ts, c), BF16),
                        pltpu.VMEM((ts, 2 * c), BF16)],
        compiler_params=pltpu.CompilerParams(
            dimension_semantics=("arbitrary", "arbitrary"), vmem_limit_bytes=VMEM_LIMIT),
        name="ab_mixer",
    )(x3, g.reshape(1, d), w_in, ln_v_g.reshape(1, c), ln_v_b.reshape(1, c), conv_w,
      conv_b.reshape(1, c), ln_a_g.reshape(1, c), ln_a_b.reshape(1, c), sp_w, sp_b.T, w_out)


def _qkv_kernel(x_ref, g_ref, w_ref, o_ref, *, d_mix):
    h = _rms(x_ref[...], g_ref[...]).astype(BF16)
    z = jnp.dot(h, w_ref[...], preferred_element_type=F32)
    o_ref[:, :d_mix] = (z[:, :d_mix] * (HEAD_DIM ** -0.5)).astype(BF16)
    o_ref[:, d_mix:] = z[:, d_mix:].astype(BF16)


def _qkv(x2, g, w_qkv, *, tm=1024):
    m, d = x2.shape
    n = w_qkv.shape[1]
    return pl.pallas_call(
        functools.partial(_qkv_kernel, d_mix=n // 3),
        out_shape=jax.ShapeDtypeStruct((m, n), BF16),
        grid=(m // tm,),
        in_specs=[
            pl.BlockSpec((tm, d), lambda i: (i, 0)),
            _resident((1, d)),
            _resident((d, n)),
        ],
        out_specs=pl.BlockSpec((tm, n), lambda i: (i, 0)),
        compiler_params=pltpu.CompilerParams(
            dimension_semantics=("arbitrary",), vmem_limit_bytes=VMEM_LIMIT),
        name="qkv",
    )(x2, g.reshape(1, d), w_qkv)


ATTN_BLOCK = 256
ATTN_ROW_CHUNK = 128
ATTN_UNROLL = 18
ATTN_LAG = 2
MASK_BIAS = -1e30
DEAD_REMAINDER = 120.0
LOG2E = 1.4426950408889634


def _wave_tables(nq):
    dummy = (0, 0, nq)
    pairs = [dummy] * ATTN_LAG
    waves = []
    for d in range(nq):
        real = [(i, i - d, i) for i in range(d, nq)]
        n = len(real) + ATTN_LAG
        n += (-n) % ATTN_UNROLL
        waves.append(((len(pairs) - ATTN_LAG) // ATTN_UNROLL, n // ATTN_UNROLL))
        pairs += real + [dummy] * (n - len(real))
    return (np.asarray(pairs, dtype=np.int32).T.copy(), np.asarray(waves, dtype=np.int32).T.copy())


def _attn_kernel(tab_ref, wave_ref, q_ref, k_ref, v_ref, bias_ref, o_ref,
                 qm_ref, sp_ref, ls_ref, a_ref, acc_ref, rem_ref):
    tq = tk = ATTN_BLOCK
    rc = ATTN_ROW_CHUNK
    seq = q_ref.shape[1]
    nq = seq // tq
    nchunk = 2 * tq // rc
    lane_q = lax.broadcasted_iota(jnp.int32, (seq, LANES), 1)
    q_all = q_ref[0]
    zero = jnp.zeros_like(q_all)
    qm_ref[0] = jnp.where(lane_q < HEAD_DIM, q_all, zero)
    qm_ref[1] = jnp.where(lane_q >= HEAD_DIM, q_all, zero)
    kr = lax.broadcasted_iota(jnp.int32, (tk, tk), 0)
    kc = lax.broadcasted_iota(jnp.int32, (tk, tk), 1)
    tri = jnp.where(kr > kc, 1.0, 0.0).astype(BF16)
    for ref in (sp_ref, ls_ref, a_ref, acc_ref, rem_ref):
        ref[...] = jnp.zeros_like(ref)

    def matmuls(t, c, r):
        rows = slice(c * rc, (c + 1) * rc)
        head, off = divmod(c * rc, tq)
        q_rows = pl.ds(pl.multiple_of(tab_ref[0, t + ATTN_LAG] * tq, tq) + off, rc)
        k_sb = k_ref[0, pl.ds(pl.multiple_of(tab_ref[1, t + ATTN_LAG] * tk, tk), tk), :]
        z = lax.dot_general(qm_ref[head, q_rows, :], k_sb, (((1,), (1,)), ((), ())),
                            preferred_element_type=F32)
        sums = jnp.dot(sp_ref[r, rows, :], tri, preferred_element_type=F32)
        v_sb = v_ref[0, pl.ds(pl.multiple_of(tab_ref[1, t] * tk, tk), tk), :]
        pv = jnp.dot(a_ref[r, rows, :], v_sb, preferred_element_type=F32)
        return z, sums, pv

    def elementwise(diagonal, t, c, w, r, z, sums, pv):
        rows = slice(c * rc, (c + 1) * rc)
        zb = z + bias_ref[rows, :] if diagonal else z
        e = jnp.exp2(jnp.abs(zb) * (-LOG2E))
        softplus = jnp.maximum(zb, 0.0) + jnp.log(1.0 + e)
        sp_ref[w, rows, :] = softplus.astype(BF16)
        ls_ref[w, rows, :] = zb - softplus
        s1 = tab_ref[2, t + ATTN_LAG - 1]
        rem = rem_ref[s1, rows, :]
        a = jnp.exp(ls_ref[r, rows, :] - sums - jnp.concatenate([rem, rem], axis=1))
        a_ref[w, rows, :] = a.astype(BF16)
        total = sums[:, 0:1] + sp_ref[r, rows, 0:1].astype(F32)
        rem_ref[s1, rows, :] = rem + jnp.broadcast_to(total, (rc, LANES))
        s2 = tab_ref[2, t]
        acc_ref[s2, rows, :] = acc_ref[s2, rows, :] + pv

    def body(diagonal, bb, carry):
        pending = None
        for u in range(ATTN_UNROLL):
            for c in range(nchunk):
                t = ATTN_UNROLL * bb + u
                w, r = u % 2, (u + 1) % 2
                vals = matmuls(t, c, r)
                if pending is not None:
                    elementwise(diagonal, *pending)
                pending = (t, c, w, r, *vals)
        elementwise(diagonal, *pending)
        return carry

    def run_wave(d, diagonal):
        first_body = wave_ref[0, d]
        lax.fori_loop(first_body, first_body + wave_ref[1, d], functools.partial(body, diagonal), 0)

    run_wave(0, True)

    def wave(state):
        d, _ = state
        run_wave(d, False)

        def block_min(i, m):
            parts = [rem_ref[i, k * tq // 4:(k + 1) * tq // 4, :] for k in range(8)]
            while len(parts) > 1:
                parts = [jnp.minimum(a, b) for a, b in zip(parts[::2], parts[1::2])]
            part = parts[0]
            rows = [part[k * SUBLANES:(k + 1) * SUBLANES, :] for k in range(tq // 4 // SUBLANES)]
            while len(rows) > 1:
                rows = [jnp.minimum(a, b) for a, b in zip(rows[::2], rows[1::2])]
            return jnp.minimum(m, rows[0])

        m = lax.fori_loop(d + 1, nq, block_min, jnp.full((SUBLANES, LANES), jnp.inf, F32))
        return d + 1, jnp.min(m) > DEAD_REMAINDER

    lax.while_loop(lambda st: jnp.logical_and(st[0] < nq, jnp.logical_not(st[1])), wave,
                   (jnp.int32(1), jnp.bool_(False)))

    lane = lax.broadcasted_iota(jnp.int32, (tq, LANES), 1)

    def write_block(i, carry):
        rows = pl.ds(pl.multiple_of(i * tq, tq), tq)
        o_ref[0, rows, :] = jnp.where(lane < HEAD_DIM, acc_ref[i, 0:tq, :],
                                      acc_ref[i, tq:, :]).astype(o_ref.dtype)
        return carry

    lax.fori_loop(0, nq, write_block, 0)


def _attention(qkv3, d_mix):
    b, s, _ = qkv3.shape
    tq = ATTN_BLOCK
    nq = s // tq
    pairs = d_mix // LANES
    assert s % tq == 0 and d_mix % LANES == 0 and ATTN_UNROLL % 2 == 0
    tab, waves = _wave_tables(nq)
    row = lax.broadcasted_iota(jnp.int32, (2 * tq, tq), 0) % tq
    col = lax.broadcasted_iota(jnp.int32, (2 * tq, tq), 1)
    bias = jnp.where(col < row, 0.0, MASK_BIAS).astype(F32)
    grid_spec = pltpu.PrefetchScalarGridSpec(
        num_scalar_prefetch=2,
        grid=(b, pairs),
        in_specs=[
            pl.BlockSpec((1, s, LANES), lambda bi, p, *_: (bi, 0, p)),
            pl.BlockSpec((1, s, LANES), lambda bi, p, *_: (bi, 0, pairs + p)),
            pl.BlockSpec((1, s, LANES), lambda bi, p, *_: (bi, 0, 2 * pairs + p)),
            pl.BlockSpec((2 * tq, tq), lambda bi, p, *_: (0, 0)),
        ],
        out_specs=pl.BlockSpec((1, s, LANES), lambda bi, p, *_: (bi, 0, p)),
        scratch_shapes=[
            pltpu.VMEM((2, s, LANES), BF16),
            pltpu.VMEM((2, 2 * tq, tq), BF16),
            pltpu.VMEM((2, 2 * tq, tq), F32),
            pltpu.VMEM((2, 2 * tq, tq), BF16),
            pltpu.VMEM((nq + 1, 2 * tq, LANES), F32),
            pltpu.VMEM((nq + 1, 2 * tq, LANES), F32),
        ],
    )
    return pl.pallas_call(
        _attn_kernel,
        out_shape=jax.ShapeDtypeStruct((b, s, d_mix), BF16),
        grid_spec=grid_spec,
        compiler_params=pltpu.CompilerParams(
            dimension_semantics=("arbitrary", "arbitrary"), vmem_limit_bytes=VMEM_LIMIT),
        name="sb_attn",
    )(jnp.asarray(tab), jnp.asarray(waves), qkv3, qkv3, qkv3, bias)


def kernel(x, g_ffn1, w_ffn1_gate, w_ffn1_up, w_ffn1_down, g_mix, w_in_ab, conv_w, conv_b, ln_a_g, ln_a_b, ln_v_g, ln_v_b, sp_w, sp_b, w_out_ab, w_qkv, w_o, g_ffn2, w_ffn2_gate, w_ffn2_up, w_ffn2_down, g_final):
    bsz, seq, d = x.shape
    depth = g_ffn1.shape[0]
    x2 = x.reshape(bsz * seq, d)
    bf = lambda w: w.astype(BF16)
    for l in range(depth):
        x2 = _ffn(x2, g_ffn1[l], bf(w_ffn1_gate[l]), bf(w_ffn1_up[l]), bf(w_ffn1_down[l]),
                  g_final, final_norm=False)
        i = l // 2
        mixer = None
        if l % 2 == 0:
            x3 = _ab(x2.reshape(bsz, seq, d), g_mix[l], bf(w_in_ab[i]), ln_v_g[i], ln_v_b[i],
                     conv_w[i], conv_b[i], ln_a_g[i], ln_a_b[i], sp_w[i], sp_b[i], bf(w_out_ab[i]))
            x2 = x3.reshape(bsz * seq, d)
        else:
            d_mix = w_qkv.shape[2] // 3
            qkv = _qkv(x2, g_mix[l], bf(w_qkv[i]))
            attn = _attention(qkv.reshape(bsz, seq, 3 * d_mix), d_mix)
            mixer = (attn.reshape(bsz * seq, d_mix), bf(w_o[i]))
        x2 = _ffn(x2, g_ffn2[l], bf(w_ffn2_gate[l]), bf(w_ffn2_up[l]), bf(w_ffn2_down[l]),
                  g_final, final_norm=(l == depth - 1), mixer=mixer)
    return x2.reshape(bsz, seq, d)
```

```python
import functools

import jax
import jax.numpy as jnp
import numpy as np
from jax import lax
from jax.experimental import pallas as pl
from jax.experimental.pallas import tpu as pltpu

EPS = 1e-6
LANES = 128
SUBLANES = 8
CONV_HALO = 32
GROUPS_B = 4
CHUNK = 128
HEAD_DIM = 64
VMEM_LIMIT = 56 * 1024 * 1024

F32 = jnp.float32
BF16 = jnp.bfloat16


def _resident(shape):
    zeros = (0,) * len(shape)
    return pl.BlockSpec(shape, lambda *_: zeros, pipeline_mode=pl.Buffered(1))


def _rms(x, g):
    ms = jnp.mean(x * x, axis=-1, keepdims=True)
    return x * lax.rsqrt(ms + EPS) * g


def _ffn_kernel(x_ref, g_ref, wg_ref, wu_ref, wd_ref, gf_ref, *rest, tf, final_norm, mixer_proj):
    if mixer_proj:
        mix_ref, wo_ref, o_ref, act_ref = rest
        x = x_ref[...] + jnp.dot(mix_ref[...], wo_ref[...], preferred_element_type=F32)
    else:
        o_ref, act_ref = rest
        x = x_ref[...]
    h = _rms(x, g_ref[...]).astype(BF16)
    d_ff = wg_ref.shape[1]
    for j in range(d_ff // tf):
        cols = slice(j * tf, (j + 1) * tf)
        gate = jnp.dot(h, wg_ref[:, cols], preferred_element_type=F32)
        up = jnp.dot(h, wu_ref[:, cols], preferred_element_type=F32)
        act_ref[:, cols] = (gate * jax.nn.sigmoid(gate) * up).astype(BF16)
    y = jnp.dot(act_ref[...], wd_ref[...], preferred_element_type=F32)
    out = x + 0.5 * y
    if final_norm:
        out = _rms(out, gf_ref[...])
    o_ref[...] = out


def _ffn(x2, g, wg, wu, wd, g_final, *, final_norm, mixer=None, tm=1024, tf=256):
    m, d = x2.shape
    d_ff = wg.shape[1]
    kern = functools.partial(_ffn_kernel, tf=tf, final_norm=final_norm,
                             mixer_proj=mixer is not None)
    in_specs = [
        pl.BlockSpec((tm, d), lambda i: (i, 0)),
        _resident((1, d)),
        _resident((d, d_ff)),
        _resident((d, d_ff)),
        _resident((d_ff, d)),
        _resident((1, d)),
    ]
    args = [x2, g.reshape(1, d), wg, wu, wd, g_final.reshape(1, d)]
    if mixer is not None:
        mix, w_o = mixer
        in_specs += [pl.BlockSpec((tm, mix.shape[1]), lambda i: (i, 0)), _resident(w_o.shape)]
        args += [mix, w_o]
    return pl.pallas_call(
        kern,
        out_shape=jax.ShapeDtypeStruct((m, d), F32),
        grid=(m // tm,),
        in_specs=in_specs,
        out_specs=pl.BlockSpec((tm, d), lambda i: (i, 0)),
        scratch_shapes=[pltpu.VMEM((tm, d_ff), BF16)],
        compiler_params=pltpu.CompilerParams(
            dimension_semantics=("arbitrary",), vmem_limit_bytes=VMEM_LIMIT),
        name="ffn",
    )(*args)


def _ab_kernel(x_ref, g_ref, wi_ref, lvg_ref, lvb_ref, cw_ref, cb_ref, lag_ref, lab_ref,
               spw_ref, spb_ref, wo_ref, o_ref, abuf_ref, shift_ref, u_ref, v_ref, m_ref):
    ts = x_ref.shape[1]
    c = cw_ref.shape[1]
    ktaps = cw_ref.shape[0]
    x = x_ref[0]
    h = _rms(x, g_ref[...]).astype(BF16)

    @pl.when(pl.program_id(1) == 0)
    def _():
        abuf_ref[0:CONV_HALO, :] = jnp.zeros((CONV_HALO, c), F32)

    for cc in range(c // LANES):
        zc = jnp.dot(h, wi_ref[:, 2 * cc * LANES:2 * (cc + 1) * LANES], preferred_element_type=F32)
        abuf_ref[CONV_HALO:, cc * LANES:(cc + 1) * LANES] = (
            zc[:, :LANES] * jax.nn.sigmoid(zc[:, LANES:]))
    zuv = jnp.dot(h, wi_ref[:, 2 * c:], preferred_element_type=F32)
    u_ref[...] = zuv[:, :c]
    lvg = lvg_ref[...]
    lvb = lvb_ref[...]
    for grp in range(GROUPS_B):
        sl = slice(grp * LANES, (grp + 1) * LANES)
        vg = zuv[:, c + grp * LANES:c + (grp + 1) * LANES]
        mu = jnp.mean(vg, axis=-1, keepdims=True)
        xc = vg - mu
        var = jnp.mean(xc * xc, axis=-1, keepdims=True)
        v_ref[:, sl] = (xc * lax.rsqrt(var + EPS) * lvg[:, sl] + lvb[:, sl]).astype(BF16)

    row = lax.broadcasted_iota(jnp.int32, (CHUNK, CHUNK), 0)
    col = lax.broadcasted_iota(jnp.int32, (CHUNK, CHUNK), 1)
    causal = row >= col
    for grp in range(GROUPS_B):
        w = jnp.where(causal, spw_ref[grp], 0.0).astype(BF16)
        bias = spb_ref[:, grp:grp + 1]
        sl = slice(grp * LANES, (grp + 1) * LANES)
        for ch in range(ts // CHUNK):
            rows = slice(ch * CHUNK, (ch + 1) * CHUNK)
            mixed = jnp.dot(w, v_ref[rows, sl], preferred_element_type=F32) + bias
            m_ref[rows, c + grp * LANES:c + (grp + 1) * LANES] = (
                u_ref[rows, sl] * mixed).astype(BF16)
    y_gate = jnp.dot(m_ref[:, c:], wo_ref[c:, :], preferred_element_type=F32)

    base = CONV_HALO - (ktaps - 1)
    span = shift_ref.shape[0]
    convs = []
    for cc in range(c // LANES):
        cols = slice(cc * LANES, (cc + 1) * LANES)
        conv = jnp.zeros((ts, LANES), F32) + cb_ref[:, cols]
        for res in range(SUBLANES):
            offs = [o for o in range(base, base + ktaps) if o % SUBLANES == res]
            if not offs:
                continue
            if res:
                shift_ref[:, cols] = abuf_ref[res:res + span, cols]
            for o in offs:
                al = o - res
                src = shift_ref[al:al + ts, cols] if res else abuf_ref[al:al + ts, cols]
                conv = conv + cw_ref[o - base:o - base + 1, cols] * src
        convs.append(conv)
    conv = jnp.concatenate(convs, axis=1)
    abuf_ref[0:CONV_HALO, :] = abuf_ref[ts:ts + CONV_HALO, :]
    mu = jnp.mean(conv, axis=-1, keepdims=True)
    xc = conv - mu
    var = jnp.mean(xc * xc, axis=-1, keepdims=True)
    a = xc * lax.rsqrt(var + EPS) * lag_ref[...] + lab_ref[...]
    m_ref[:, 0:c] = (a * jax.nn.sigmoid(a)).astype(BF16)

    y = jnp.dot(m_ref[:, 0:c], wo_ref[0:c, :], preferred_element_type=F32)
    o_ref[0] = x + (y + y_gate)


def _ab(x3, g, w_in, ln_v_g, ln_v_b, conv_w, conv_b, ln_a_g, ln_a_b, sp_w, sp_b, w_out, *, ts=512):
    b, s, d = x3.shape
    c = w_in.shape[1] // 4
    ktaps = conv_w.shape[0]
    assert ktaps - 1 <= CONV_HALO and CONV_HALO % SUBLANES == 0 and s % ts == 0 and ts % CHUNK == 0
    assert c == GROUPS_B * LANES
    val = w_in[:, :c].reshape(d, c // LANES, LANES)
    gate = w_in[:, c:2 * c].reshape(d, c // LANES, LANES)
    w_glu = jnp.concatenate([jnp.stack([val, gate], axis=2).reshape(d, 2 * c), w_in[:, 2 * c:]], axis=1)
    return pl.pallas_call(
        _ab_kernel,
        out_shape=jax.ShapeDtypeStruct((b, s, d), F32),
        grid=(b, s // ts),
        in_specs=[
            pl.BlockSpec((1, ts, d), lambda bi, i: (bi, i, 0)),
            _resident((1, d)),
            _resident((d, 4 * c)),
            _resident((1, c)),
            _resident((1, c)),
            _resident((ktaps, c)),
            _resident((1, c)),
            _resident((1, c)),
            _resident((1, c)),
            _resident((GROUPS_B, CHUNK, CHUNK)),
            _resident((CHUNK, GROUPS_B)),
            _resident((2 * c, d)),
        ],
        out_specs=pl.BlockSpec((1, ts, d), lambda bi, i: (bi, i, 0)),
        scratch_shapes=[pltpu.VMEM((ts + CONV_HALO, c), F32),
                        pltpu.VMEM((ts + CONV_HALO - SUBLANES, c), F32),
                        pltpu.VMEM((ts, c), F32),
                        pltpu.VMEM((ts, c), BF16),
                        pltpu.VMEM((ts, 2 * c), BF16)],
        compiler_params=pltpu.CompilerParams(
            dimension_semantics=("arbitrary", "arbitrary"), vmem_limit_bytes=VMEM_LIMIT),
        name="ab_mixer",
    )(x3, g.reshape(1, d), w_glu, ln_v_g.reshape(1, c), ln_v_b.reshape(1, c), conv_w,
      conv_b.reshape(1, c), ln_a_g.reshape(1, c), ln_a_b.reshape(1, c), sp_w, sp_b.T, w_out)


def _qkv_kernel(x_ref, g_ref, w_ref, o_ref, *, d_mix):
    h = _rms(x_ref[...], g_ref[...]).astype(BF16)
    z = jnp.dot(h, w_ref[...], preferred_element_type=F32)
    o_ref[:, :d_mix] = (z[:, :d_mix] * (HEAD_DIM ** -0.5)).astype(BF16)
    o_ref[:, d_mix:] = z[:, d_mix:].astype(BF16)


def _qkv(x2, g, w_qkv, *, tm=1024):
    m, d = x2.shape
    n = w_qkv.shape[1]
    return pl.pallas_call(
        functools.partial(_qkv_kernel, d_mix=n // 3),
        out_shape=jax.ShapeDtypeStruct((m, n), BF16),
        grid=(m // tm,),
        in_specs=[
            pl.BlockSpec((tm, d), lambda i: (i, 0)),
            _resident((1, d)),
            _resident((d, n)),
        ],
        out_specs=pl.BlockSpec((tm, n), lambda i: (i, 0)),
        compiler_params=pltpu.CompilerParams(
            dimension_semantics=("arbitrary",), vmem_limit_bytes=VMEM_LIMIT),
        name="qkv",
    )(x2, g.reshape(1, d), w_qkv)


ATTN_BLOCK = 256
ATTN_ROW_CHUNK = 128
ATTN_UNROLL = 18
ATTN_LAG = 2
MASK_BIAS = -1e30
DEAD_REMAINDER = 120.0
LOG2E = 1.4426950408889634


def _wave_tables(nq):
    dummy = (0, 0, nq)
    pairs = [dummy] * ATTN_LAG
    waves = []
    for d in range(nq):
        real = [(i, i - d, i) for i in range(d, nq)]
        n = len(real) + ATTN_LAG
        n += (-n) % ATTN_UNROLL
        waves.append(((len(pairs) - ATTN_LAG) // ATTN_UNROLL, n // ATTN_UNROLL))
        pairs += real + [dummy] * (n - len(real))
    return (np.asarray(pairs, dtype=np.int32).T.copy(), np.asarray(waves, dtype=np.int32).T.copy())


def _attn_kernel(tab_ref, wave_ref, q_ref, k_ref, v_ref, bias_ref, o_ref,
                 qm_ref, sp_ref, ls_ref, a_ref, acc_ref, rem_ref):
    tq = tk = ATTN_BLOCK
    rc = ATTN_ROW_CHUNK
    seq = q_ref.shape[1]
    nq = seq // tq
    nchunk = 2 * tq // rc
    lane_q = lax.broadcasted_iota(jnp.int32, (seq, LANES), 1)
    q_all = q_ref[0]
    zero = jnp.zeros_like(q_all)
    qm_ref[0] = jnp.where(lane_q < HEAD_DIM, q_all, zero)
    qm_ref[1] = jnp.where(lane_q >= HEAD_DIM, q_all, zero)
    kr = lax.broadcasted_iota(jnp.int32, (tk, tk), 0)
    kc = lax.broadcasted_iota(jnp.int32, (tk, tk), 1)
    tri = jnp.where(kr > kc, 1.0, 0.0).astype(BF16)
    for ref in (sp_ref, ls_ref, a_ref, acc_ref, rem_ref):
        ref[...] = jnp.zeros_like(ref)

    def matmuls(t, c, r):
        rows = slice(c * rc, (c + 1) * rc)
        head, off = divmod(c * rc, tq)
        q_rows = pl.ds(pl.multiple_of(tab_ref[0, t + ATTN_LAG] * tq, tq) + off, rc)
        k_sb = k_ref[0, pl.ds(pl.multiple_of(tab_ref[1, t + ATTN_LAG] * tk, tk), tk), :]
        z = lax.dot_general(qm_ref[head, q_rows, :], k_sb, (((1,), (1,)), ((), ())),
                            preferred_element_type=F32)
        sums = jnp.dot(sp_ref[r, rows, :], tri, preferred_element_type=F32)
        v_sb = v_ref[0, pl.ds(pl.multiple_of(tab_ref[1, t] * tk, tk), tk), :]
        pv = jnp.dot(a_ref[r, rows, :], v_sb, preferred_element_type=F32)
        return z, sums, pv

    def elementwise(diagonal, t, c, w, r, z, sums, pv):
        rows = slice(c * rc, (c + 1) * rc)
        zb = z + bias_ref[rows, :] if diagonal else z
        e = jnp.exp2(jnp.abs(zb) * (-LOG2E))
        softplus = jnp.maximum(zb, 0.0) + jnp.log(1.0 + e)
        sp_ref[w, rows, :] = softplus.astype(BF16)
        ls_ref[w, rows, :] = zb - softplus
        s1 = tab_ref[2, t + ATTN_LAG - 1]
        rem = rem_ref[s1, rows, :]
        a = jnp.exp(ls_ref[r, rows, :] - sums - jnp.concatenate([rem, rem], axis=1))
        a_ref[w, rows, :] = a.astype(BF16)
        total = sums[:, 0:1] + sp_ref[r, rows, 0:1].astype(F32)
        rem_ref[s1, rows, :] = rem + jnp.broadcast_to(total, (rc, LANES))
        s2 = tab_ref[2, t]
        acc_ref[s2, rows, :] = acc_ref[s2, rows, :] + pv

    def body(diagonal, bb, carry):
        pending = None
        for u in range(ATTN_UNROLL):
            for c in range(nchunk):
                t = ATTN_UNROLL * bb + u
                w, r = u % 2, (u + 1) % 2
                vals = matmuls(t, c, r)
                if pending is not None:
                    elementwise(diagonal, *pending)
                pending = (t, c, w, r, *vals)
        elementwise(diagonal, *pending)
        return carry

    def run_wave(d, diagonal):
        first_body = wave_ref[0, d]
        lax.fori_loop(first_body, first_body + wave_ref[1, d], functools.partial(body, diagonal), 0)

    run_wave(0, True)

    def wave(state):
        d, _ = state
        run_wave(d, False)

        def block_min(i, m):
            parts = [rem_ref[i, k * tq // 4:(k + 1) * tq // 4, :] for k in range(8)]
            while len(parts) > 1:
                parts = [jnp.minimum(a, b) for a, b in zip(parts[::2], parts[1::2])]
            part = parts[0]
            rows = [part[k * SUBLANES:(k + 1) * SUBLANES, :] for k in range(tq // 4 // SUBLANES)]
            while len(rows) > 1:
                rows = [jnp.minimum(a, b) for a, b in zip(rows[::2], rows[1::2])]
            return jnp.minimum(m, rows[0])

        m = lax.fori_loop(d + 1, nq, block_min, jnp.full((SUBLANES, LANES), jnp.inf, F32))
        return d + 1, jnp.min(m) > DEAD_REMAINDER

    lax.while_loop(lambda st: jnp.logical_and(st[0] < nq, jnp.logical_not(st[1])), wave,
                   (jnp.int32(1), jnp.bool_(False)))

    lane = lax.broadcasted_iota(jnp.int32, (tq, LANES), 1)

    def write_block(i, carry):
        rows = pl.ds(pl.multiple_of(i * tq, tq), tq)
        o_ref[0, rows, :] = jnp.where(lane < HEAD_DIM, acc_ref[i, 0:tq, :],
                                      acc_ref[i, tq:, :]).astype(o_ref.dtype)
        return carry

    lax.fori_loop(0, nq, write_block, 0)


def _attention(qkv3, d_mix):
    b, s, _ = qkv3.shape
    tq = ATTN_BLOCK
    nq = s // tq
    pairs = d_mix // LANES
    assert s % tq == 0 and d_mix % LANES == 0 and ATTN_UNROLL % 2 == 0
    tab, waves = _wave_tables(nq)
    row = lax.broadcasted_iota(jnp.int32, (2 * tq, tq), 0) % tq
    col = lax.broadcasted_iota(jnp.int32, (2 * tq, tq), 1)
    bias = jnp.where(col < row, 0.0, MASK_BIAS).astype(F32)
    grid_spec = pltpu.PrefetchScalarGridSpec(
        num_scalar_prefetch=2,
        grid=(b, pairs),
        in_specs=[
            pl.BlockSpec((1, s, LANES), lambda bi, p, *_: (bi, 0, p)),
            pl.BlockSpec((1, s, LANES), lambda bi, p, *_: (bi, 0, pairs + p)),
            pl.BlockSpec((1, s, LANES), lambda bi, p, *_: (bi, 0, 2 * pairs + p)),
            pl.BlockSpec((2 * tq, tq), lambda bi, p, *_: (0, 0)),
        ],
        out_specs=pl.BlockSpec((1, s, LANES), lambda bi, p, *_: (bi, 0, p)),
        scratch_shapes=[
            pltpu.VMEM((2, s, LANES), BF16),
            pltpu.VMEM((2, 2 * tq, tq), BF16),
            pltpu.VMEM((2, 2 * tq, tq), F32),
            pltpu.VMEM((2, 2 * tq, tq), BF16),
            pltpu.VMEM((nq + 1, 2 * tq, LANES), F32),
            pltpu.VMEM((nq + 1, 2 * tq, LANES), F32),
        ],
    )
    return pl.pallas_call(
        _attn_kernel,
        out_shape=jax.ShapeDtypeStruct((b, s, d_mix), BF16),
        grid_spec=grid_spec,
        compiler_params=pltpu.CompilerParams(
            dimension_semantics=("arbitrary", "arbitrary"), vmem_limit_bytes=VMEM_LIMIT),
        name="sb_attn",
    )(jnp.asarray(tab), jnp.asarray(waves), qkv3, qkv3, qkv3, bias)


def kernel(x, g_ffn1, w_ffn1_gate, w_ffn1_up, w_ffn1_down, g_mix, w_in_ab, conv_w, conv_b, ln_a_g, ln_a_b, ln_v_g, ln_v_b, sp_w, sp_b, w_out_ab, w_qkv, w_o, g_ffn2, w_ffn2_gate, w_ffn2_up, w_ffn2_down, g_final):
    bsz, seq, d = x.shape
    depth = g_ffn1.shape[0]
    x2 = x.reshape(bsz * seq, d)
    bf = lambda w: w.astype(BF16)
    for l in range(depth):
        x2 = _ffn(x2, g_ffn1[l], bf(w_ffn1_gate[l]), bf(w_ffn1_up[l]), bf(w_ffn1_down[l]),
                  g_final, final_norm=False)
        i = l // 2
        mixer = None
        if l % 2 == 0:
            x3 = _ab(x2.reshape(bsz, seq, d), g_mix[l], bf(w_in_ab[i]), ln_v_g[i], ln_v_b[i],
                     conv_w[i], conv_b[i], ln_a_g[i], ln_a_b[i], sp_w[i], sp_b[i], bf(w_out_ab[i]))
            x2 = x3.reshape(bsz * seq, d)
        else:
            d_mix = w_qkv.shape[2] // 3
            qkv = _qkv(x2, g_mix[l], bf(w_qkv[i]))
            attn = _attention(qkv.reshape(bsz, seq, 3 * d_mix), d_mix)
            mixer = (attn.reshape(bsz * seq, d_mix), bf(w_o[i]))
        x2 = _ffn(x2, g_ffn2[l], bf(w_ffn2_gate[l]), bf(w_ffn2_up[l]), bf(w_ffn2_down[l]),
                  g_final, final_norm=(l == depth - 1), mixer=mixer)
    return x2.reshape(bsz, seq, d)
```

```python
import functools

import jax
import jax.numpy as jnp
import numpy as np
from jax import lax
from jax.experimental import pallas as pl
from jax.experimental.pallas import tpu as pltpu

EPS = 1e-6
LANES = 128
SUBLANES = 8
CONV_HALO = 32
GROUPS_B = 4
CHUNK = 128
HEAD_DIM = 64
VMEM_LIMIT = 56 * 1024 * 1024

F32 = jnp.float32
BF16 = jnp.bfloat16


def _resident(shape):
    zeros = (0,) * len(shape)
    return pl.BlockSpec(shape, lambda *_: zeros, pipeline_mode=pl.Buffered(1))


def _rms(x, g):
    ms = jnp.mean(x * x, axis=-1, keepdims=True)
    return x * lax.rsqrt(ms + EPS) * g


def _ffn_kernel(x_ref, g_ref, wg_ref, wu_ref, wd_ref, gf_ref, *rest, tf, final_norm, mixer_proj):
    if mixer_proj:
        mix_ref, wo_ref, o_ref, act_ref = rest
        x = x_ref[...] + jnp.dot(mix_ref[...], wo_ref[...], preferred_element_type=F32)
    else:
        o_ref, act_ref = rest
        x = x_ref[...]
    h = _rms(x, g_ref[...]).astype(BF16)
    d_ff = wg_ref.shape[1]
    for j in range(d_ff // tf):
        cols = slice(j * tf, (j + 1) * tf)
        gate = jnp.dot(h, wg_ref[:, cols], preferred_element_type=F32)
        up = jnp.dot(h, wu_ref[:, cols], preferred_element_type=F32)
        act_ref[:, cols] = (gate * jax.nn.sigmoid(gate) * up).astype(BF16)
    y = jnp.dot(act_ref[...], wd_ref[...], preferred_element_type=F32)
    out = x + 0.5 * y
    if final_norm:
        out = _rms(out, gf_ref[...])
    o_ref[...] = out


def _ffn(x2, g, wg, wu, wd, g_final, *, final_norm, mixer=None, tm=1024, tf=256):
    m, d = x2.shape
    d_ff = wg.shape[1]
    kern = functools.partial(_ffn_kernel, tf=tf, final_norm=final_norm,
                             mixer_proj=mixer is not None)
    in_specs = [
        pl.BlockSpec((tm, d), lambda i: (i, 0)),
        _resident((1, d)),
        _resident((d, d_ff)),
        _resident((d, d_ff)),
        _resident((d_ff, d)),
        _resident((1, d)),
    ]
    args = [x2, g.reshape(1, d), wg, wu, wd, g_final.reshape(1, d)]
    if mixer is not None:
        mix, w_o = mixer
        in_specs += [pl.BlockSpec((tm, mix.shape[1]), lambda i: (i, 0)), _resident(w_o.shape)]
        args += [mix, w_o]
    return pl.pallas_call(
        kern,
        out_shape=jax.ShapeDtypeStruct((m, d), F32),
        grid=(m // tm,),
        in_specs=in_specs,
        out_specs=pl.BlockSpec((tm, d), lambda i: (i, 0)),
        scratch_shapes=[pltpu.VMEM((tm, d_ff), BF16)],
        compiler_params=pltpu.CompilerParams(
            dimension_semantics=("arbitrary",), vmem_limit_bytes=VMEM_LIMIT),
        name="ffn",
    )(*args)


def _ab_kernel(x_ref, g_ref, wi_ref, lvg_ref, lvb_ref, cw_ref, cb_ref, lag_ref, lab_ref,
               spw_ref, spb_ref, wo_ref, o_ref, abuf_ref, shift_ref, u_ref, v_ref, m_ref):
    ts = x_ref.shape[1]
    c = cw_ref.shape[1]
    ktaps = cw_ref.shape[0]
    x = x_ref[0]
    h = _rms(x, g_ref[...]).astype(BF16)

    @pl.when(pl.program_id(1) == 0)
    def _():
        abuf_ref[0:CONV_HALO, :] = jnp.zeros((CONV_HALO, c), F32)

    for cc in range(c // LANES):
        zc = jnp.dot(h, wi_ref[:, 2 * cc * LANES:2 * (cc + 1) * LANES], preferred_element_type=F32)
        abuf_ref[CONV_HALO:, cc * LANES:(cc + 1) * LANES] = (
            zc[:, :LANES] * jax.nn.sigmoid(zc[:, LANES:]))
    zuv = jnp.dot(h, wi_ref[:, 2 * c:], preferred_element_type=F32)
    u_ref[...] = zuv[:, :c]
    lvg = lvg_ref[...]
    lvb = lvb_ref[...]
    for grp in range(GROUPS_B):
        sl = slice(grp * LANES, (grp + 1) * LANES)
        vg = zuv[:, c + grp * LANES:c + (grp + 1) * LANES]
        mu = jnp.mean(vg, axis=-1, keepdims=True)
        xc = vg - mu
        var = jnp.mean(xc * xc, axis=-1, keepdims=True)
        v_ref[:, sl] = (xc * lax.rsqrt(var + EPS) * lvg[:, sl] + lvb[:, sl]).astype(BF16)

    row = lax.broadcasted_iota(jnp.int32, (CHUNK, CHUNK), 0)
    col = lax.broadcasted_iota(jnp.int32, (CHUNK, CHUNK), 1)
    causal = row >= col
    for grp in range(GROUPS_B):
        w = jnp.where(causal, spw_ref[grp], 0.0).astype(BF16)
        bias = spb_ref[:, grp:grp + 1]
        sl = slice(grp * LANES, (grp + 1) * LANES)
        for ch in range(ts // CHUNK):
            rows = slice(ch * CHUNK, (ch + 1) * CHUNK)
            mixed = jnp.dot(w, v_ref[rows, sl], preferred_element_type=F32) + bias
            m_ref[rows, c + grp * LANES:c + (grp + 1) * LANES] = (
                u_ref[rows, sl] * mixed).astype(BF16)
    y_gate = jnp.dot(m_ref[:, c:], wo_ref[c:, :], preferred_element_type=F32)

    base = CONV_HALO - (ktaps - 1)
    span = shift_ref.shape[0]
    convs = []
    for cc in range(c // LANES):
        cols = slice(cc * LANES, (cc + 1) * LANES)
        conv = jnp.zeros((ts, LANES), F32) + cb_ref[:, cols]
        for res in range(SUBLANES):
            offs = [o for o in range(base, base + ktaps) if o % SUBLANES == res]
            if not offs:
                continue
            if res:
                shift_ref[:, cols] = abuf_ref[res:res + span, cols]
            for o in offs:
                al = o - res
                src = shift_ref[al:al + ts, cols] if res else abuf_ref[al:al + ts, cols]
                conv = conv + cw_ref[o - base:o - base + 1, cols] * src
        convs.append(conv)
    conv = jnp.concatenate(convs, axis=1)
    abuf_ref[0:CONV_HALO, :] = abuf_ref[ts:ts + CONV_HALO, :]
    mu = jnp.mean(conv, axis=-1, keepdims=True)
    xc = conv - mu
    var = jnp.mean(xc * xc, axis=-1, keepdims=True)
    a = xc * lax.rsqrt(var + EPS) * lag_ref[...] + lab_ref[...]
    m_ref[:, 0:c] = (a * jax.nn.sigmoid(a)).astype(BF16)

    y = jnp.dot(m_ref[:, 0:c], wo_ref[0:c, :], preferred_element_type=F32)
    o_ref[0] = x + (y + y_gate)


def _ab(x3, g, w_in, ln_v_g, ln_v_b, conv_w, conv_b, ln_a_g, ln_a_b, sp_w, sp_b, w_out, *, ts=512):
    b, s, d = x3.shape
    c = w_in.shape[1] // 4
    ktaps = conv_w.shape[0]
    assert ktaps - 1 <= CONV_HALO and CONV_HALO % SUBLANES == 0 and s % ts == 0 and ts % CHUNK == 0
    assert c == GROUPS_B * LANES
    val = w_in[:, :c].reshape(d, c // LANES, LANES)
    gate = w_in[:, c:2 * c].reshape(d, c // LANES, LANES)
    w_glu = jnp.concatenate([jnp.stack([val, gate], axis=2).reshape(d, 2 * c), w_in[:, 2 * c:]], axis=1)
    return pl.pallas_call(
        _ab_kernel,
        out_shape=jax.ShapeDtypeStruct((b, s, d), F32),
        grid=(b, s // ts),
        in_specs=[
            pl.BlockSpec((1, ts, d), lambda bi, i: (bi, i, 0)),
            _resident((1, d)),
            _resident((d, 4 * c)),
            _resident((1, c)),
            _resident((1, c)),
            _resident((ktaps, c)),
            _resident((1, c)),
            _resident((1, c)),
            _resident((1, c)),
            _resident((GROUPS_B, CHUNK, CHUNK)),
            _resident((CHUNK, GROUPS_B)),
            _resident((2 * c, d)),
        ],
        out_specs=pl.BlockSpec((1, ts, d), lambda bi, i: (bi, i, 0)),
        scratch_shapes=[pltpu.VMEM((ts + CONV_HALO, c), F32),
                        pltpu.VMEM((ts + CONV_HALO - SUBLANES, c), F32),
                        pltpu.VMEM((ts, c), F32),
                        pltpu.VMEM((ts, c), BF16),
                        pltpu.VMEM((ts, 2 * c), BF16)],
        compiler_params=pltpu.CompilerParams(
            dimension_semantics=("arbitrary", "arbitrary"), vmem_limit_bytes=VMEM_LIMIT),
        name="ab_mixer",
    )(x3, g.reshape(1, d), w_glu, ln_v_g.reshape(1, c), ln_v_b.reshape(1, c), conv_w,
      conv_b.reshape(1, c), ln_a_g.reshape(1, c), ln_a_b.reshape(1, c), sp_w, sp_b.T, w_out)


def _qkv_kernel(x_ref, g_ref, w_ref, o_ref, *, d_mix):
    h = _rms(x_ref[...], g_ref[...]).astype(BF16)
    z = jnp.dot(h, w_ref[...], preferred_element_type=F32)
    o_ref[:, :d_mix] = (z[:, :d_mix] * (HEAD_DIM ** -0.5)).astype(BF16)
    o_ref[:, d_mix:] = z[:, d_mix:].astype(BF16)


def _qkv(x2, g, w_qkv, *, tm=1024):
    m, d = x2.shape
    n = w_qkv.shape[1]
    return pl.pallas_call(
        functools.partial(_qkv_kernel, d_mix=n // 3),
        out_shape=jax.ShapeDtypeStruct((m, n), BF16),
        grid=(m // tm,),
        in_specs=[
            pl.BlockSpec((tm, d), lambda i: (i, 0)),
            _resident((1, d)),
            _resident((d, n)),
        ],
        out_specs=pl.BlockSpec((tm, n), lambda i: (i, 0)),
        compiler_params=pltpu.CompilerParams(
            dimension_semantics=("arbitrary",), vmem_limit_bytes=VMEM_LIMIT),
        name="qkv",
    )(x2, g.reshape(1, d), w_qkv)


ATTN_BLOCK = 256
ATTN_ROW_CHUNK = 128
ATTN_UNROLL = 18
ATTN_LAG = 2
MASK_BIAS = -1e30
DEAD_REMAINDER = 120.0
LOG2E = 1.4426950408889634


def _wave_tables(nq):
    dummy = (0, 0, nq)
    pairs = [dummy] * ATTN_LAG
    waves = []
    for d in range(nq):
        real = [(i, i - d, i) for i in range(d, nq)]
        n = len(real) + ATTN_LAG
        n += (-n) % ATTN_UNROLL
        waves.append(((len(pairs) - ATTN_LAG) // ATTN_UNROLL, n // ATTN_UNROLL))
        pairs += real + [dummy] * (n - len(real))
    return (np.asarray(pairs, dtype=np.int32).T.copy(), np.asarray(waves, dtype=np.int32).T.copy())


def _attn_kernel(tab_ref, wave_ref, q_ref, k_ref, v_ref, bias_ref, o_ref,
                 qm_ref, sp_ref, ls_ref, a_ref, acc_ref, rem_ref):
    tq = tk = ATTN_BLOCK
    rc = ATTN_ROW_CHUNK
    seq = q_ref.shape[1]
    nq = seq // tq
    nchunk = 2 * tq // rc
    lane_q = lax.broadcasted_iota(jnp.int32, (seq, LANES), 1)
    q_all = q_ref[0]
    zero = jnp.zeros_like(q_all)
    qm_ref[0] = jnp.where(lane_q < HEAD_DIM, q_all, zero)
    qm_ref[1] = jnp.where(lane_q >= HEAD_DIM, q_all, zero)
    kr = lax.broadcasted_iota(jnp.int32, (tk, tk), 0)
    kc = lax.broadcasted_iota(jnp.int32, (tk, tk), 1)
    tri = jnp.where(kr > kc, 1.0, 0.0).astype(BF16)
    for ref in (sp_ref, ls_ref, a_ref):
        ref[...] = jnp.zeros_like(ref)

    def matmuls(t, c, r):
        rows = slice(c * rc, (c + 1) * rc)
        head, off = divmod(c * rc, tq)
        q_rows = pl.ds(pl.multiple_of(tab_ref[0, t + ATTN_LAG] * tq, tq) + off, rc)
        k_sb = k_ref[0, pl.ds(pl.multiple_of(tab_ref[1, t + ATTN_LAG] * tk, tk), tk), :]
        z = lax.dot_general(qm_ref[head, q_rows, :], k_sb, (((1,), (1,)), ((), ())),
                            preferred_element_type=F32)
        sums = jnp.dot(sp_ref[r, rows, :], tri, preferred_element_type=F32)
        v_sb = v_ref[0, pl.ds(pl.multiple_of(tab_ref[1, t] * tk, tk), tk), :]
        pv = jnp.dot(a_ref[r, rows, :], v_sb, preferred_element_type=F32)
        return z, sums, pv

    def elementwise(diagonal, t, c, w, r, z, sums, pv):
        rows = slice(c * rc, (c + 1) * rc)
        zb = z + bias_ref[rows, :] if diagonal else z
        e = jnp.exp2(jnp.abs(zb) * (-LOG2E))
        softplus = jnp.maximum(zb, 0.0) + jnp.log(1.0 + e)
        sp_ref[w, rows, :] = softplus.astype(BF16)
        ls_ref[w, rows, :] = zb - softplus
        s1 = tab_ref[2, t + ATTN_LAG - 1]
        total = sums[:, 0:1] + sp_ref[r, rows, 0:1].astype(F32)
        total = jnp.broadcast_to(total, (rc, LANES))
        if diagonal:
            a = jnp.exp(ls_ref[r, rows, :] - sums)
            rem_ref[s1, rows, :] = total
        else:
            rem = rem_ref[s1, rows, :]
            a = jnp.exp(ls_ref[r, rows, :] - sums - jnp.concatenate([rem, rem], axis=1))
            rem_ref[s1, rows, :] = rem + total
        a_ref[w, rows, :] = a.astype(BF16)
        s2 = tab_ref[2, t]
        acc_ref[s2, rows, :] = pv if diagonal else acc_ref[s2, rows, :] + pv

    def body(diagonal, bb, carry):
        pending = None
        for u in range(ATTN_UNROLL):
            for c in range(nchunk):
                t = ATTN_UNROLL * bb + u
                w, r = u % 2, (u + 1) % 2
                vals = matmuls(t, c, r)
                if pending is not None:
                    elementwise(diagonal, *pending)
                pending = (t, c, w, r, *vals)
        elementwise(diagonal, *pending)
        return carry

    def run_wave(d, diagonal):
        first_body = wave_ref[0, d]
        lax.fori_loop(first_body, first_body + wave_ref[1, d], functools.partial(body, diagonal), 0)

    run_wave(0, True)

    def wave(state):
        d, _ = state
        run_wave(d, False)

        def block_min(i, m):
            parts = [rem_ref[i, k * tq // 4:(k + 1) * tq // 4, :] for k in range(8)]
            while len(parts) > 1:
                parts = [jnp.minimum(a, b) for a, b in zip(parts[::2], parts[1::2])]
            part = parts[0]
            rows = [part[k * SUBLANES:(k + 1) * SUBLANES, :] for k in range(tq // 4 // SUBLANES)]
            while len(rows) > 1:
                rows = [jnp.minimum(a, b) for a, b in zip(rows[::2], rows[1::2])]
            return jnp.minimum(m, rows[0])

        m = lax.fori_loop(d + 1, nq, block_min, jnp.full((SUBLANES, LANES), jnp.inf, F32))
        return d + 1, jnp.min(m) > DEAD_REMAINDER

    lax.while_loop(lambda st: jnp.logical_and(st[0] < nq, jnp.logical_not(st[1])), wave,
                   (jnp.int32(1), jnp.bool_(False)))

    lane = lax.broadcasted_iota(jnp.int32, (tq, LANES), 1)

    def write_block(i, carry):
        rows = pl.ds(pl.multiple_of(i * tq, tq), tq)
        o_ref[0, rows, :] = jnp.where(lane < HEAD_DIM, acc_ref[i, 0:tq, :],
                                      acc_ref[i, tq:, :]).astype(o_ref.dtype)
        return carry

    lax.fori_loop(0, nq, write_block, 0)


def _attention(qkv3, d_mix):
    b, s, _ = qkv3.shape
    tq = ATTN_BLOCK
    nq = s // tq
    pairs = d_mix // LANES
    assert s % tq == 0 and d_mix % LANES == 0 and ATTN_UNROLL % 2 == 0
    tab, waves = _wave_tables(nq)
    row = lax.broadcasted_iota(jnp.int32, (2 * tq, tq), 0) % tq
    col = lax.broadcasted_iota(jnp.int32, (2 * tq, tq), 1)
    bias = jnp.where(col < row, 0.0, MASK_BIAS).astype(F32)
    grid_spec = pltpu.PrefetchScalarGridSpec(
        num_scalar_prefetch=2,
        grid=(b, pairs),
        in_specs=[
            pl.BlockSpec((1, s, LANES), lambda bi, p, *_: (bi, 0, p)),
            pl.BlockSpec((1, s, LANES), lambda bi, p, *_: (bi, 0, pairs + p)),
            pl.BlockSpec((1, s, LANES), lambda bi, p, *_: (bi, 0, 2 * pairs + p)),
            pl.BlockSpec((2 * tq, tq), lambda bi, p, *_: (0, 0)),
        ],
        out_specs=pl.BlockSpec((1, s, LANES), lambda bi, p, *_: (bi, 0, p)),
        scratch_shapes=[
            pltpu.VMEM((2, s, LANES), BF16),
            pltpu.VMEM((2, 2 * tq, tq), BF16),
            pltpu.VMEM((2, 2 * tq, tq), F32),
            pltpu.VMEM((2, 2 * tq, tq), BF16),
            pltpu.VMEM((nq + 1, 2 * tq, LANES), F32),
            pltpu.VMEM((nq + 1, 2 * tq, LANES), F32),
        ],
    )
    return pl.pallas_call(
        _attn_kernel,
        out_shape=jax.ShapeDtypeStruct((b, s, d_mix), BF16),
        grid_spec=grid_spec,
        compiler_params=pltpu.CompilerParams(
            dimension_semantics=("arbitrary", "arbitrary"), vmem_limit_bytes=VMEM_LIMIT),
        name="sb_attn",
    )(jnp.asarray(tab), jnp.asarray(waves), qkv3, qkv3, qkv3, bias)


def kernel(x, g_ffn1, w_ffn1_gate, w_ffn1_up, w_ffn1_down, g_mix, w_in_ab, conv_w, conv_b, ln_a_g, ln_a_b, ln_v_g, ln_v_b, sp_w, sp_b, w_out_ab, w_qkv, w_o, g_ffn2, w_ffn2_gate, w_ffn2_up, w_ffn2_down, g_final):
    bsz, seq, d = x.shape
    depth = g_ffn1.shape[0]
    x2 = x.reshape(bsz * seq, d)
    bf = lambda w: w.astype(BF16)
    for l in range(depth):
        x2 = _ffn(x2, g_ffn1[l], bf(w_ffn1_gate[l]), bf(w_ffn1_up[l]), bf(w_ffn1_down[l]),
                  g_final, final_norm=False)
        i = l // 2
        mixer = None
        if l % 2 == 0:
            x3 = _ab(x2.reshape(bsz, seq, d), g_mix[l], bf(w_in_ab[i]), ln_v_g[i], ln_v_b[i],
                     conv_w[i], conv_b[i], ln_a_g[i], ln_a_b[i], sp_w[i], sp_b[i], bf(w_out_ab[i]))
            x2 = x3.reshape(bsz * seq, d)
        else:
            d_mix = w_qkv.shape[2] // 3
            qkv = _qkv(x2, g_mix[l], bf(w_qkv[i]))
            attn = _attention(qkv.reshape(bsz, seq, 3 * d_mix), d_mix)
            mixer = (attn.reshape(bsz * seq, d_mix), bf(w_o[i]))
        x2 = _ffn(x2, g_ffn2[l], bf(w_ffn2_gate[l]), bf(w_ffn2_up[l]), bf(w_ffn2_down[l]),
                  g_final, final_norm=(l == depth - 1), mixer=mixer)
    return x2.reshape(bsz, seq, d)
```
